```python
import math
import jax, jax.numpy as jnp
from jax import lax
import numpy as np

D_MODEL = 1024
BATCH = 16
SEQ = 2048
DEPTH = 2

D_MIX = D_MODEL
W_POOL = D_MIX // 4
W_SCONV = D_MIX // 4
W_CONF = D_MIX // 4
W_ATTN = D_MIX // 4
POOL_WINDOWS = (2, 4, 8, 16)
N_POOL_GROUPS = len(POOL_WINDOWS)
POOL_GROUP = W_POOL // N_POOL_GROUPS
SCONV_K = 3
CONF_K = 31
N_HEADS = 4
HEAD_V = W_ATTN // N_HEADS
HEAD_QK = HEAD_V // 2
SPLIT_SIZES = (W_POOL, W_SCONV, W_SCONV, W_SCONV, W_CONF, W_CONF, W_ATTN, W_ATTN, W_ATTN)
D_IN = sum(SPLIT_SIZES)
D_FF = ((8 * D_MODEL + 3 * 256 - 1) // (3 * 256)) * 256
REL_BUCKETS = 32
REL_MAX_EXACT = REL_BUCKETS // 2
REL_MAX_DIST = 128
Q_BLOCK = 128
EPS = 1e-6
NEG_INF = -1e30

kernel_name = 'hybrid_parallel_pool_conv_conformer_diffattn'


def rms_norm(x, g):
    xf = x.astype(jnp.float32)
    y = xf * lax.rsqrt(jnp.mean(xf * xf, axis=-1, keepdims=True) + EPS)
    return (y * g.astype(jnp.float32)).astype(x.dtype)


def layer_norm(x, g, b):
    xf = x.astype(jnp.float32)
    mu = jnp.mean(xf, axis=-1, keepdims=True)
    xc = xf - mu
    y = xc * lax.rsqrt(jnp.mean(xc * xc, axis=-1, keepdims=True) + EPS)
    return (y * g.astype(jnp.float32) + b.astype(jnp.float32)).astype(x.dtype)


def causal_dwconv(u, w):
    k = w.shape[0]
    return lax.conv_general_dilated(
        u, w[:, None, :].astype(u.dtype), window_strides=(1,), padding=[(k - 1, 0)],
        dimension_numbers=('NWC', 'WIO', 'NWC'), feature_group_count=u.shape[-1])


def pool_mixer(u, pool_w, pool_scale):
    b, s, _ = u.shape
    c = jnp.cumsum(u.astype(jnp.float32), axis=1)
    pos1 = jnp.arange(1, s + 1)
    parts = []
    for g, win in enumerate(POOL_WINDOWS):
        cg = c[..., g * POOL_GROUP:(g + 1) * POOL_GROUP]
        shifted = jnp.pad(cg, ((0, 0), (win, 0), (0, 0)))[:, :s]
        cnt = jnp.minimum(pos1, win).astype(jnp.float32)[:, None]
        parts.append((cg - shifted) / cnt)
    pooled = jnp.concatenate(parts, axis=-1).astype(u.dtype) - u
    y = jnp.einsum('bsgc,gcd->bsgd', pooled.reshape(b, s, N_POOL_GROUPS, POOL_GROUP), pool_w)
    return y.reshape(b, s, W_POOL) * pool_scale


def rel_bucket(n):
    nf = jnp.maximum(n, 1).astype(jnp.float32)
    large = REL_MAX_EXACT + (jnp.log(nf / REL_MAX_EXACT) / math.log(REL_MAX_DIST / REL_MAX_EXACT)
                             * (REL_BUCKETS - REL_MAX_EXACT)).astype(jnp.int32)
    large = jnp.minimum(large, REL_BUCKETS - 1)
    return jnp.where(n < REL_MAX_EXACT, n, large)


def diff_attention(q, k, v, q_norm_g, k_norm_g, lam, lam_init, subln_g, rel_bias):
    b, s = q.shape[:2]
    q = rms_norm(q, q_norm_g)
    k = rms_norm(k, k_norm_g)
    scale = HEAD_QK ** -0.5
    nb = s // Q_BLOCK
    qb = q.reshape(b, nb, Q_BLOCK, N_HEADS, 2, HEAD_QK).transpose(1, 0, 2, 3, 4, 5)
    starts = jnp.arange(nb, dtype=jnp.int32) * Q_BLOCK
    kpos = jnp.arange(s, dtype=jnp.int32)

    def block(args):
        qblk, start = args
        qpos = start + jnp.arange(Q_BLOCK, dtype=jnp.int32)
        dist = qpos[:, None] - kpos[None, :]
        bias = rel_bias[rel_bucket(jnp.maximum(dist, 0))].astype(jnp.float32)
        bias = bias.transpose(2, 0, 1)[:, None]
        sc = jnp.einsum('bqhmd,bkhmd->bhmqk', qblk, k).astype(jnp.float32) * scale + bias
        sc = jnp.where(dist >= 0, sc, NEG_INF)
        p = jax.nn.softmax(sc, axis=-1)
        a = p[:, :, 0] - lam * p[:, :, 1]
        return jnp.einsum('bhqk,bkhd->bqhd', a.astype(v.dtype), v)

    o = lax.map(block, (qb, starts))
    o = o.transpose(1, 0, 2, 3, 4).reshape(b, s, N_HEADS, HEAD_V)
    o = rms_norm(o, subln_g) * (1.0 - lam_init)
    return o.reshape(b, s, W_ATTN)


def hybrid_layer(x, norm1_g, w_in, pool_w, pool_scale, sconv_w, conf_dw_w, conf_dw_b,
                 conf_ln_g, conf_ln_b, q_norm_g, k_norm_g, lam_q1, lam_k1, lam_q2, lam_k2,
                 subln_g, w_out, norm2_g, w_gate, w_up, w_down, rel_bias, lam_init):
    b, s, _ = x.shape
    h = rms_norm(x, norm1_g)
    z = h @ w_in
    idx = [int(i) for i in np.cumsum(SPLIT_SIZES)[:-1]]
    z_pool, s_b, s_c, s_h, c_a, c_g, zq, zk, zv = jnp.split(z, idx, axis=-1)
    y_pool = pool_mixer(z_pool, pool_w, pool_scale)
    y_sconv = s_b * causal_dwconv(s_c * s_h, sconv_w)
    u = c_a * jax.nn.sigmoid(c_g)
    u = causal_dwconv(u, conf_dw_w) + conf_dw_b
    y_conf = jax.nn.silu(layer_norm(u, conf_ln_g, conf_ln_b))
    lam = (jnp.exp(jnp.sum(lam_q1.astype(jnp.float32) * lam_k1.astype(jnp.float32)))
           - jnp.exp(jnp.sum(lam_q2.astype(jnp.float32) * lam_k2.astype(jnp.float32))) + lam_init)
    q = zq.reshape(b, s, N_HEADS, 2, HEAD_QK)
    k = zk.reshape(b, s, N_HEADS, 2, HEAD_QK)
    v = zv.reshape(b, s, N_HEADS, HEAD_V)
    y_attn = diff_attention(q, k, v, q_norm_g, k_norm_g, lam, lam_init, subln_g, rel_bias)
    mix = jnp.concatenate([y_pool, y_sconv, y_conf, y_attn], axis=-1)
    x = x + mix @ w_out
    h2 = rms_norm(x, norm2_g)
    x = x + (jax.nn.silu(h2 @ w_gate) * (h2 @ w_up)) @ w_down
    return x


def setup_inputs(seed: int = 0) -> dict:
    key = jax.random.key(seed)
    ks = jax.random.split(key, 24)
    f32 = jnp.float32
    nrm = lambda k, shape, sc: jax.random.normal(k, shape, f32) * sc
    L = DEPTH
    return {
        'x': nrm(ks[0], (BATCH, SEQ, D_MODEL), 1.0),
        'norm1_g': 1.0 + nrm(ks[1], (L, D_MODEL), 0.05),
        'w_in': nrm(ks[2], (L, D_MODEL, D_IN), D_MODEL ** -0.5),
        'pool_w': nrm(ks[3], (L, N_POOL_GROUPS, POOL_GROUP, POOL_GROUP), POOL_GROUP ** -0.5),
        'pool_scale': 1.0 + nrm(ks[4], (L, W_POOL), 0.1),
        'sconv_w': nrm(ks[5], (L, SCONV_K, W_SCONV), SCONV_K ** -0.5),
        'conf_dw_w': nrm(ks[6], (L, CONF_K, W_CONF), CONF_K ** -0.5),
        'conf_dw_b': nrm(ks[7], (L, W_CONF), 0.02),
        'conf_ln_g': 1.0 + nrm(ks[8], (L, W_CONF), 0.05),
        'conf_ln_b': nrm(ks[9], (L, W_CONF), 0.02),
        'q_norm_g': 1.0 + nrm(ks[10], (L, HEAD_QK), 0.05),
        'k_norm_g': 1.0 + nrm(ks[11], (L, HEAD_QK), 0.05),
        'lam_q1': nrm(ks[12], (L, HEAD_QK), 0.1),
        'lam_k1': nrm(ks[13], (L, HEAD_QK), 0.1),
        'lam_q2': nrm(ks[14], (L, HEAD_QK), 0.1),
        'lam_k2': nrm(ks[15], (L, HEAD_QK), 0.1),
        'subln_g': 1.0 + nrm(ks[16], (L, HEAD_V), 0.05),
        'w_out': nrm(ks[17], (L, D_MIX, D_MODEL), D_MIX ** -0.5),
        'norm2_g': 1.0 + nrm(ks[18], (L, D_MODEL), 0.05),
        'w_gate': nrm(ks[19], (L, D_MODEL, D_FF), D_MODEL ** -0.5),
        'w_up': nrm(ks[20], (L, D_MODEL, D_FF), D_MODEL ** -0.5),
        'w_down': nrm(ks[21], (L, D_FF, D_MODEL), D_FF ** -0.5),
        'rel_bias': nrm(ks[22], (REL_BUCKETS, N_HEADS), 0.5),
    }


def reference(x, norm1_g, w_in, pool_w, pool_scale, sconv_w, conf_dw_w, conf_dw_b, conf_ln_g,
              conf_ln_b, q_norm_g, k_norm_g, lam_q1, lam_k1, lam_q2, lam_k2, subln_g, w_out,
              norm2_g, w_gate, w_up, w_down, rel_bias):
    for l in range(DEPTH):
        lam_init = 0.8 - 0.6 * math.exp(-0.3 * l)
        x = hybrid_layer(x, norm1_g[l], w_in[l], pool_w[l], pool_scale[l], sconv_w[l],
                         conf_dw_w[l], conf_dw_b[l], conf_ln_g[l], conf_ln_b[l], q_norm_g[l],
                         k_norm_g[l], lam_q1[l], lam_k1[l], lam_q2[l], lam_k2[l], subln_g[l],
                         w_out[l], norm2_g[l], w_gate[l], w_up[l], w_down[l], rel_bias, lam_init)
    return x
```

```python
import functools
import math

import numpy as np
import jax
import jax.numpy as jnp
from jax import lax
from jax.experimental import pallas as pl
from jax.experimental.pallas import tpu as pltpu

F32 = jnp.float32
BF16 = jnp.bfloat16

D_MODEL = 1024
GROUP_W = 256
N_SLICES = 9
D_IN = N_SLICES * GROUP_W
N_MIX_SLICES = 6
POOL_WINDOWS = (2, 4, 8, 16)
POOL_GROUP = GROUP_W // len(POOL_WINDOWS)
SCONV_K = 3
CONF_K = 31
N_HEADS = 4
HEAD_V = GROUP_W // N_HEADS
HEAD_QK = HEAD_V // 2
N_MAPS = 2 * N_HEADS
D_FF = 2816
REL_BUCKETS = 32
REL_MAX_EXACT = 16
REL_MAX_DIST = 128
EPS = 1e-6
NEG_INF = -1e30
LOG2E = math.log2(math.e)

HALO = 32
VMEM_LIMIT = 56 * 1024 * 1024


def _bucket_starts(max_dist):
    n = np.arange(0, max_dist + 1)
    nf = np.maximum(n, 1).astype(np.float32)
    large = REL_MAX_EXACT + (np.log(nf / np.float32(REL_MAX_EXACT)) / np.float32(math.log(REL_MAX_DIST / REL_MAX_EXACT))
                             * np.float32(REL_BUCKETS - REL_MAX_EXACT)).astype(np.int32)
    large = np.minimum(large, REL_BUCKETS - 1)
    bucket = np.where(n < REL_MAX_EXACT, n, large)
    starts = [int(np.argmax(bucket >= b)) for b in range(REL_BUCKETS)]
    assert bucket[-1] == REL_BUCKETS - 1
    return starts


def _in_proj_kernel(x_ref, g_ref, w_ref, z_ref):
    x = x_ref[...]
    ms = jnp.mean(x * x, axis=-1, keepdims=True)
    h = x * lax.rsqrt(ms + EPS) * g_ref[...]
    z = jnp.dot(h.astype(BF16), w_ref[...], preferred_element_type=F32)
    z_ref[...] = z.astype(z_ref.dtype)


def _in_proj(x2d, g, w, tm):
    t = x2d.shape[0]
    return pl.pallas_call(
        _in_proj_kernel,
        grid=(t // tm,),
        in_specs=[
            pl.BlockSpec((tm, D_MODEL), lambda i: (i, 0)),
            pl.BlockSpec((1, D_MODEL), lambda i: (0, 0)),
            pl.BlockSpec((D_MODEL, D_IN), lambda i: (0, 0)),
        ],
        out_specs=pl.BlockSpec((tm, D_IN), lambda i: (i, 0)),
        out_shape=jax.ShapeDtypeStruct((t, D_IN), BF16),
        compiler_params=pltpu.CompilerParams(
            dimension_semantics=("arbitrary",), vmem_limit_bytes=VMEM_LIMIT),
        name="in_proj",
    )(x2d, g, w)


def _mixer_kernel(zc_ref, zh_ref, poolw_ref, pscale_ref, sconvw_ref, cw_ref, cb_ref, lng_ref, lnb_ref,
                  out_ref, a0, b1, b2, b3, p_ext, u_ext, *, ts, row_chunk):
    first = pl.program_id(1) == 0
    keep = jnp.where(first, 0.0, 1.0).astype(F32)
    n = HALO + ts
    W = GROUP_W

    def cur(k):
        return zc_ref[:, k * W:(k + 1) * W].astype(F32)

    def halo(k):
        return zh_ref[:, k * W:(k + 1) * W].astype(F32) * keep

    zp = cur(0)
    a0[0:HALO, :] = halo(0)
    a0[HALO:n, :] = zp
    b1[8:n, :] = a0[8:n, :] + a0[7:n - 1, :]
    b2[16:n, :] = b1[16:n, :] + b1[14:n - 2, :]
    b3[24:n, :] = b2[24:n, :] + b2[20:n - 4, :]
    s16 = b3[HALO:n, :] + b3[HALO - 8:n - 8, :]
    lane = lax.broadcasted_iota(jnp.int32, (ts, W), 1)
    pos1 = pl.program_id(1) * ts + lax.broadcasted_iota(jnp.int32, (ts, W), 0) + 1
    sel = jnp.where(lane < POOL_GROUP, b1[HALO:n, :],
                    jnp.where(lane < 2 * POOL_GROUP, b2[HALO:n, :],
                              jnp.where(lane < 3 * POOL_GROUP, b3[HALO:n, :], s16)))
    win = jnp.where(lane < POOL_GROUP, POOL_WINDOWS[0],
                    jnp.where(lane < 2 * POOL_GROUP, POOL_WINDOWS[1],
                              jnp.where(lane < 3 * POOL_GROUP, POOL_WINDOWS[2], POOL_WINDOWS[3])))
    cnt = jnp.minimum(pos1, win).astype(F32)
    pooled = sel / cnt - zp
    y_pool = jnp.dot(pooled.astype(BF16), poolw_ref[...], preferred_element_type=F32) * pscale_ref[...]
    out_ref[:, 0:W] = y_pool.astype(out_ref.dtype)

    p_ext[0:8, :] = (zh_ref[HALO - 8:HALO, 2 * W:3 * W].astype(F32)
                     * zh_ref[HALO - 8:HALO, 3 * W:4 * W].astype(F32)) * keep
    p_ext[8:8 + ts, :] = cur(2) * cur(3)
    conv3 = sconvw_ref[SCONV_K - 1:SCONV_K, :] * p_ext[8:8 + ts, :]
    for j in range(1, SCONV_K):
        conv3 = conv3 + sconvw_ref[SCONV_K - 1 - j:SCONV_K - j, :] * p_ext[8 - j:8 - j + ts, :]
    out_ref[:, W:2 * W] = (cur(1) * conv3).astype(out_ref.dtype)

    u_ext[0:HALO, :] = halo(4) * jax.nn.sigmoid(zh_ref[:, 5 * W:6 * W].astype(F32))
    u_ext[HALO:n, :] = cur(4) * jax.nn.sigmoid(cur(5))
    for r0 in range(0, ts, row_chunk):
        acc = cb_ref[...] + cw_ref[CONF_K - 1:CONF_K, :] * u_ext[HALO + r0:HALO + r0 + row_chunk, :]
        for j in range(1, CONF_K):
            acc = acc + cw_ref[CONF_K - 1 - j:CONF_K - j, :] * u_ext[HALO + r0 - j:HALO + r0 - j + row_chunk, :]
        mu = jnp.mean(acc, axis=-1, keepdims=True)
        xc = acc - mu
        var = jnp.mean(xc * xc, axis=-1, keepdims=True)
        yn = xc * lax.rsqrt(var + EPS) * lng_ref[...] + lnb_ref[...]
        out_ref[r0:r0 + row_chunk, 2 * W:3 * W] = (yn * jax.nn.sigmoid(yn)).astype(out_ref.dtype)


def _mixers(z, poolw_bd, pscale, sconvw, cw, cb, lng, lnb, batch, seq, ts):
    t = z.shape[0]
    nst = seq // ts
    hb = ts // HALO
    mix_w = N_MIX_SLICES * GROUP_W
    const = lambda b, s: (0, 0)
    kern = functools.partial(_mixer_kernel, ts=ts, row_chunk=64)
    return pl.pallas_call(
        kern,
        grid=(batch, nst),
        in_specs=[
            pl.BlockSpec((ts, mix_w), lambda b, s: (b * nst + s, 0)),
            pl.BlockSpec((HALO, mix_w), lambda b, s: (jnp.maximum((b * nst + s) * hb - 1, 0), 0)),
            pl.BlockSpec((GROUP_W, GROUP_W), const),
            pl.BlockSpec((1, GROUP_W), const),
            pl.BlockSpec((SCONV_K, GROUP_W), const),
            pl.BlockSpec((CONF_K, GROUP_W), const),
            pl.BlockSpec((1, GROUP_W), const),
            pl.BlockSpec((1, GROUP_W), const),
            pl.BlockSpec((1, GROUP_W), const),
        ],
        out_specs=pl.BlockSpec((ts, 3 * GROUP_W), lambda b, s: (b * nst + s, 0)),
        out_shape=jax.ShapeDtypeStruct((t, 3 * GROUP_W), BF16),
        scratch_shapes=[
            pltpu.VMEM((HALO + ts, GROUP_W), F32),
            pltpu.VMEM((HALO + ts, GROUP_W), F32),
            pltpu.VMEM((HALO + ts, GROUP_W), F32),
            pltpu.VMEM((HALO + ts, GROUP_W), F32),
            pltpu.VMEM((8 + ts, GROUP_W), F32),
            pltpu.VMEM((HALO + ts, GROUP_W), F32),
        ],
        compiler_params=pltpu.CompilerParams(
            dimension_semantics=("arbitrary", "arbitrary"), vmem_limit_bytes=VMEM_LIMIT),
        name="mixers",
    )(z, z, poolw_bd, pscale, sconvw, cw, cb, lng, lnb)


def _group_rms(x, gsum_ref):
    x2 = x * x
    hi = x2.astype(BF16)
    lo = (x2 - hi.astype(F32)).astype(BF16)
    ssq = (jnp.dot(hi, gsum_ref[...], preferred_element_type=F32)
           + jnp.dot(lo, gsum_ref[...], preferred_element_type=F32))
    return lax.rsqrt(ssq * (1.0 / HEAD_QK) + EPS)


def _attn_kernel(tbl_ref, q_ref, k_ref, v_ref, gq_ref, gk_ref, gsum_ref, lam_ref, subg_ref,
                 o_ref, kt_scr, vh_scr, bias_scr, m_scr, l_scr, acc_scr, *, tq, lam_init, starts):
    b = pl.program_id(0)
    qi = pl.program_id(1)
    tk = tq
    seq = k_ref.shape[0]

    @pl.when((b == 0) & (qi == 0))
    def _():
        r = lax.broadcasted_iota(jnp.int32, (tq, tk), 0)
        c = lax.broadcasted_iota(jnp.int32, (tq, tk), 1)
        for off in range(2):
            d = r - c + off * tk
            for h in range(N_HEADS):
                val = jnp.full((tq, tk), tbl_ref[h * REL_BUCKETS], F32)
                for bk in range(1, REL_BUCKETS):
                    val = jnp.where(d >= starts[bk], tbl_ref[h * REL_BUCKETS + bk], val)
                if off == 0:
                    val = jnp.where(d >= 0, val, NEG_INF)
                bias_scr[off * N_HEADS + h] = val

    @pl.when(qi == 0)
    def _():
        for c0 in range(0, seq, tk):
            kf = k_ref[c0:c0 + tk, :].astype(F32)
            kn = kf * _group_rms(kf, gsum_ref) * gk_ref[...]
            kt_scr[:, c0:c0 + tk] = kn.T.astype(BF16)
            for h in range(N_HEADS):
                vh_scr[h, c0:c0 + tk, :] = v_ref[c0:c0 + tk, h * HEAD_V:(h + 1) * HEAD_V]

    qf = q_ref[...].astype(F32)
    qn = (qf * _group_rms(qf, gsum_ref) * gq_ref[...]).astype(BF16)

    lp = lam_ref[...]
    lam = (jnp.exp(jnp.sum(lp[0:1] * lp[1:2], axis=-1, keepdims=True))
           - jnp.exp(jnp.sum(lp[2:3] * lp[3:4], axis=-1, keepdims=True)) + lam_init)

    for h in range(N_HEADS):
        far_bias = tbl_ref[h * REL_BUCKETS + REL_BUCKETS - 1]
        for mp in range(2):
            g = 2 * h + mp
            q_g = qn[:, g * HEAD_QK:(g + 1) * HEAD_QK]
            m_scr[...] = jnp.full((tq, 1), NEG_INF, F32)
            l_scr[...] = jnp.zeros((tq, 1), F32)
            acc_scr[...] = jnp.zeros((tq, HEAD_V), F32)

            def step(j, bias):
                c0 = pl.multiple_of(j * tk, tk)
                s = jnp.dot(q_g, kt_scr[g * HEAD_QK:(g + 1) * HEAD_QK, pl.ds(c0, tk)],
                            preferred_element_type=F32) + bias
                m_old = m_scr[...]
                m_new = jnp.maximum(m_old, jnp.max(s, axis=-1, keepdims=True))
                alpha = jnp.exp2(m_old - m_new)
                p = jnp.exp2(s - m_new)
                l_scr[...] = alpha * l_scr[...] + jnp.sum(p, axis=-1, keepdims=True)
                acc_scr[...] = alpha * acc_scr[...] + jnp.dot(
                    p.astype(BF16), vh_scr[h, pl.ds(c0, tk), :], preferred_element_type=F32)
                m_scr[...] = m_new

            def far_body(j, carry):
                step(j, far_bias)
                return carry

            lax.fori_loop(0, qi - 1, far_body, 0)

            @pl.when(qi >= 1)
            def _():
                step(qi - 1, bias_scr[N_HEADS + h])

            step(qi, bias_scr[h])

            res = acc_scr[...] / l_scr[...]
            if mp == 0:
                o_h = res
            else:
                o_h = o_h - lam * res
        ms = jnp.mean(o_h * o_h, axis=-1, keepdims=True)
        y = o_h * lax.rsqrt(ms + EPS) * subg_ref[...] * (1.0 - lam_init)
        o_ref[:, h * HEAD_V:(h + 1) * HEAD_V] = y.astype(o_ref.dtype)


def _attention(z, tbl, gq, gk, gsum, lam_params, subg, batch, seq, tq, lam_init):
    t = z.shape[0]
    nq = seq // tq
    starts = _bucket_starts(2 * tq)
    assert starts[REL_BUCKETS - 1] <= tq, "blocks two or more tiles below the diagonal must sit in the last bucket"
    const = lambda b, i: (0, 0)
    kern = functools.partial(_attn_kernel, tq=tq, lam_init=lam_init, starts=starts)
    return pl.pallas_call(
        kern,
        grid=(batch, nq),
        in_specs=[
            pl.BlockSpec(memory_space=pltpu.SMEM),
            pl.BlockSpec((tq, GROUP_W), lambda b, i: (b * nq + i, 6)),
            pl.BlockSpec((seq, GROUP_W), lambda b, i: (b, 7)),
            pl.BlockSpec((seq, GROUP_W), lambda b, i: (b, 8)),
            pl.BlockSpec((1, GROUP_W), const),
            pl.BlockSpec((1, GROUP_W), const),
            pl.BlockSpec((GROUP_W, GROUP_W), const),
            pl.BlockSpec((4, HEAD_QK), const),
            pl.BlockSpec((1, HEAD_V), const),
        ],
        out_specs=pl.BlockSpec((tq, GROUP_W), lambda b, i: (b * nq + i, 0)),
        out_shape=jax.ShapeDtypeStruct((t, GROUP_W), BF16),
        scratch_shapes=[
            pltpu.VMEM((GROUP_W, seq), BF16),
            pltpu.VMEM((N_HEADS, seq, HEAD_V), BF16),
            pltpu.VMEM((2 * N_HEADS, tq, tq), F32),
            pltpu.VMEM((tq, 1), F32),
            pltpu.VMEM((tq, 1), F32),
            pltpu.VMEM((tq, HEAD_V), F32),
        ],
        compiler_params=pltpu.CompilerParams(
            dimension_semantics=("arbitrary", "arbitrary"), vmem_limit_bytes=VMEM_LIMIT),
        name="diff_attention",
    )(tbl, z, z, z, gq, gk, gsum, lam_params, subg)


def _out_ffn_kernel(x_ref, mix_ref, att_ref, wo_ref, g2_ref, wg_ref, wu_ref, wd_ref, o_ref, a_scr, *, ff_chunk):
    mix_w = mix_ref.shape[1]
    x1 = (x_ref[...]
          + jnp.dot(mix_ref[...], wo_ref[0:mix_w, :], preferred_element_type=F32)
          + jnp.dot(att_ref[...], wo_ref[mix_w:, :], preferred_element_type=F32))
    ms = jnp.mean(x1 * x1, axis=-1, keepdims=True)
    h2 = (x1 * lax.rsqrt(ms + EPS) * g2_ref[...]).astype(BF16)
    for c0 in range(0, D_FF, ff_chunk):
        gate = jnp.dot(h2, wg_ref[:, c0:c0 + ff_chunk], preferred_element_type=F32)
        up = jnp.dot(h2, wu_ref[:, c0:c0 + ff_chunk], preferred_element_type=F32)
        a_scr[:, c0:c0 + ff_chunk] = (gate * jax.nn.sigmoid(gate) * up).astype(BF16)
    o_ref[...] = x1 + jnp.dot(a_scr[...], wd_ref[...], preferred_element_type=F32)


def _out_ffn(x2d, mix, att, wo, g2, wg, wu, wd, tm):
    t = x2d.shape[0]
    const = lambda i: (0, 0)
    single = pl.Buffered(1)
    kern = functools.partial(_out_ffn_kernel, ff_chunk=256)
    return pl.pallas_call(
        kern,
        grid=(t // tm,),
        in_specs=[
            pl.BlockSpec((tm, D_MODEL), lambda i: (i, 0)),
            pl.BlockSpec((tm, mix.shape[1]), lambda i: (i, 0)),
            pl.BlockSpec((tm, att.shape[1]), lambda i: (i, 0)),
            pl.BlockSpec((D_MODEL, D_MODEL), const, pipeline_mode=single),
            pl.BlockSpec((1, D_MODEL), const),
            pl.BlockSpec((D_MODEL, D_FF), const, pipeline_mode=single),
            pl.BlockSpec((D_MODEL, D_FF), const, pipeline_mode=single),
            pl.BlockSpec((D_FF, D_MODEL), const, pipeline_mode=single),
        ],
        out_specs=pl.BlockSpec((tm, D_MODEL), lambda i: (i, 0)),
        out_shape=jax.ShapeDtypeStruct((t, D_MODEL), F32),
        scratch_shapes=[pltpu.VMEM((tm, D_FF), BF16)],
        compiler_params=pltpu.CompilerParams(
            dimension_semantics=("arbitrary",), vmem_limit_bytes=VMEM_LIMIT),
        name="out_ffn",
    )(x2d, mix, att, wo, g2, wg, wu, wd)


def _block_diag_pool(pool_w):
    out = jnp.zeros((GROUP_W, GROUP_W), pool_w.dtype)
    for g in range(len(POOL_WINDOWS)):
        out = out.at[g * POOL_GROUP:(g + 1) * POOL_GROUP, g * POOL_GROUP:(g + 1) * POOL_GROUP].set(pool_w[g])
    return out


def kernel(x, norm1_g, w_in, pool_w, pool_scale, sconv_w, conf_dw_w, conf_dw_b, conf_ln_g, conf_ln_b,
           q_norm_g, k_norm_g, lam_q1, lam_k1, lam_q2, lam_k2, subln_g, w_out, norm2_g, w_gate, w_up,
           w_down, rel_bias):
    batch, seq, d = x.shape
    assert d == D_MODEL and w_in.shape[-1] == D_IN
    depth = w_in.shape[0]
    t = batch * seq
    tm = min(512, t)
    ts = min(256, seq)
    tq = min(256, seq)
    assert t % tm == 0 and seq % ts == 0 and seq % tq == 0 and ts % HALO == 0

    gid = np.arange(GROUP_W) // HEAD_QK
    gsum = jnp.asarray((gid[:, None] == gid[None, :]).astype(np.float32), BF16)
    tbl = (rel_bias.astype(F32).T * LOG2E).reshape(-1)
    row = lambda v: v.astype(F32).reshape(1, -1)

    x2d = x.reshape(t, D_MODEL)
    for l in range(depth):
        lam_init = 0.8 - 0.6 * math.exp(-0.3 * l)
        z = _in_proj(x2d, row(norm1_g[l]), w_in[l].astype(BF16), tm)
        mix = _mixers(z, _block_diag_pool(pool_w[l]).astype(BF16), row(pool_scale[l]), sconv_w[l].astype(F32),
                      conf_dw_w[l].astype(F32), row(conf_dw_b[l]), row(conf_ln_g[l]), row(conf_ln_b[l]),
                      batch, seq, ts)
        gq = row(jnp.tile(q_norm_g[l].astype(F32), N_MAPS) * (HEAD_QK ** -0.5 * LOG2E))
        gk = row(jnp.tile(k_norm_g[l].astype(F32), N_MAPS))
        lam_params = jnp.stack([lam_q1[l], lam_k1[l], lam_q2[l], lam_k2[l]]).astype(F32)
        att = _attention(z, tbl, gq, gk, gsum, lam_params, row(subln_g[l]), batch, seq, tq, lam_init)
        x2d = _out_ffn(x2d, mix, att, w_out[l].astype(BF16), row(norm2_g[l]), w_gate[l].astype(BF16),
                       w_up[l].astype(BF16), w_down[l].astype(BF16), tm)
    return x2d.reshape(batch, seq, D_MODEL)
```

```python
import functools
import math

import numpy as np
import jax
import jax.numpy as jnp
from jax import lax
from jax.experimental import pallas as pl
from jax.experimental.pallas import tpu as pltpu

F32 = jnp.float32
BF16 = jnp.bfloat16

D_MODEL = 1024
GROUP_W = 256
N_SLICES = 9
D_IN = N_SLICES * GROUP_W
N_MIX_SLICES = 6
POOL_WINDOWS = (2, 4, 8, 16)
POOL_GROUP = GROUP_W // len(POOL_WINDOWS)
SCONV_K = 3
CONF_K = 31
N_HEADS = 4
HEAD_V = GROUP_W // N_HEADS
HEAD_QK = HEAD_V // 2
N_MAPS = 2 * N_HEADS
D_FF = 2816
REL_BUCKETS = 32
REL_MAX_EXACT = 16
REL_MAX_DIST = 128
EPS = 1e-6
NEG_INF = -1e30
LOG2E = math.log2(math.e)

HALO = 32
VMEM_LIMIT = 56 * 1024 * 1024


def _bucket_starts(max_dist):
    n = np.arange(0, max_dist + 1)
    nf = np.maximum(n, 1).astype(np.float32)
    large = REL_MAX_EXACT + (np.log(nf / np.float32(REL_MAX_EXACT)) / np.float32(math.log(REL_MAX_DIST / REL_MAX_EXACT))
                             * np.float32(REL_BUCKETS - REL_MAX_EXACT)).astype(np.int32)
    large = np.minimum(large, REL_BUCKETS - 1)
    bucket = np.where(n < REL_MAX_EXACT, n, large)
    starts = [int(np.argmax(bucket >= b)) for b in range(REL_BUCKETS)]
    assert bucket[-1] == REL_BUCKETS - 1
    return starts


def _in_proj_kernel(x_ref, g_ref, w_ref, z_ref):
    x = x_ref[...]
    ms = jnp.mean(x * x, axis=-1, keepdims=True)
    h = x * lax.rsqrt(ms + EPS) * g_ref[...]
    z = jnp.dot(h.astype(BF16), w_ref[...], preferred_element_type=F32)
    z_ref[...] = z.astype(z_ref.dtype)


def _in_proj(x2d, g, w, tm):
    t = x2d.shape[0]
    return pl.pallas_call(
        _in_proj_kernel,
        grid=(t // tm,),
        in_specs=[
            pl.BlockSpec((tm, D_MODEL), lambda i: (i, 0)),
            pl.BlockSpec((1, D_MODEL), lambda i: (0, 0)),
            pl.BlockSpec((D_MODEL, D_IN), lambda i: (0, 0)),
        ],
        out_specs=pl.BlockSpec((tm, D_IN), lambda i: (i, 0)),
        out_shape=jax.ShapeDtypeStruct((t, D_IN), BF16),
        compiler_params=pltpu.CompilerParams(
            dimension_semantics=("arbitrary",), vmem_limit_bytes=VMEM_LIMIT),
        name="in_proj",
    )(x2d, g, w)


def _mixer_kernel(zc_ref, zh_ref, poolw_ref, pscale_ref, sconvw_ref, cw_ref, cb_ref, lng_ref, lnb_ref,
                  out_ref, a0, b1, b2, b3, p_ext, u_ext, *, ts, row_chunk):
    first = pl.program_id(1) == 0
    keep = jnp.where(first, 0.0, 1.0).astype(F32)
    n = HALO + ts
    W = GROUP_W

    def cur(k):
        return zc_ref[:, k * W:(k + 1) * W].astype(F32)

    def halo(k):
        return zh_ref[:, k * W:(k + 1) * W].astype(F32) * keep

    zp = cur(0)
    a0[0:HALO, :] = halo(0)
    a0[HALO:n, :] = zp
    b1[8:n, :] = a0[8:n, :] + a0[7:n - 1, :]
    b2[16:n, :] = b1[16:n, :] + b1[14:n - 2, :]
    b3[24:n, :] = b2[24:n, :] + b2[20:n - 4, :]
    s16 = b3[HALO:n, :] + b3[HALO - 8:n - 8, :]
    lane = lax.broadcasted_iota(jnp.int32, (ts, W), 1)
    pos1 = pl.program_id(1) * ts + lax.broadcasted_iota(jnp.int32, (ts, W), 0) + 1
    sel = jnp.where(lane < POOL_GROUP, b1[HALO:n, :],
                    jnp.where(lane < 2 * POOL_GROUP, b2[HALO:n, :],
                              jnp.where(lane < 3 * POOL_GROUP, b3[HALO:n, :], s16)))
    win = jnp.where(lane < POOL_GROUP, POOL_WINDOWS[0],
                    jnp.where(lane < 2 * POOL_GROUP, POOL_WINDOWS[1],
                              jnp.where(lane < 3 * POOL_GROUP, POOL_WINDOWS[2], POOL_WINDOWS[3])))
    cnt = jnp.minimum(pos1, win).astype(F32)
    pooled = sel / cnt - zp
    y_pool = jnp.dot(pooled.astype(BF16), poolw_ref[...], preferred_element_type=F32) * pscale_ref[...]
    out_ref[:, 0:W] = y_pool.astype(out_ref.dtype)

    p_ext[0:8, :] = (zh_ref[HALO - 8:HALO, 2 * W:3 * W].astype(F32)
                     * zh_ref[HALO - 8:HALO, 3 * W:4 * W].astype(F32)) * keep
    p_ext[8:8 + ts, :] = cur(2) * cur(3)
    conv3 = sconvw_ref[SCONV_K - 1:SCONV_K, :] * p_ext[8:8 + ts, :]
    for j in range(1, SCONV_K):
        conv3 = conv3 + sconvw_ref[SCONV_K - 1 - j:SCONV_K - j, :] * p_ext[8 - j:8 - j + ts, :]
    out_ref[:, W:2 * W] = (cur(1) * conv3).astype(out_ref.dtype)

    u_ext[0:HALO, :] = halo(4) * jax.nn.sigmoid(zh_ref[:, 5 * W:6 * W].astype(F32))
    u_ext[HALO:n, :] = cur(4) * jax.nn.sigmoid(cur(5))
    for r0 in range(0, ts, row_chunk):
        acc = cb_ref[...] + cw_ref[CONF_K - 1:CONF_K, :] * u_ext[HALO + r0:HALO + r0 + row_chunk, :]
        for j in range(1, CONF_K):
            acc = acc + cw_ref[CONF_K - 1 - j:CONF_K - j, :] * u_ext[HALO + r0 - j:HALO + r0 - j + row_chunk, :]
        mu = jnp.mean(acc, axis=-1, keepdims=True)
        xc = acc - mu
        var = jnp.mean(xc * xc, axis=-1, keepdims=True)
        yn = xc * lax.rsqrt(var + EPS) * lng_ref[...] + lnb_ref[...]
        out_ref[r0:r0 + row_chunk, 2 * W:3 * W] = (yn * jax.nn.sigmoid(yn)).astype(out_ref.dtype)


def _mixers(z, poolw_bd, pscale, sconvw, cw, cb, lng, lnb, batch, seq, ts):
    t = z.shape[0]
    nst = seq // ts
    hb = ts // HALO
    mix_w = N_MIX_SLICES * GROUP_W
    const = lambda b, s: (0, 0)
    kern = functools.partial(_mixer_kernel, ts=ts, row_chunk=64)
    return pl.pallas_call(
        kern,
        grid=(batch, nst),
        in_specs=[
            pl.BlockSpec((ts, mix_w), lambda b, s: (b * nst + s, 0)),
            pl.BlockSpec((HALO, mix_w), lambda b, s: (jnp.maximum((b * nst + s) * hb - 1, 0), 0)),
            pl.BlockSpec((GROUP_W, GROUP_W), const),
            pl.BlockSpec((1, GROUP_W), const),
            pl.BlockSpec((SCONV_K, GROUP_W), const),
            pl.BlockSpec((CONF_K, GROUP_W), const),
            pl.BlockSpec((1, GROUP_W), const),
            pl.BlockSpec((1, GROUP_W), const),
            pl.BlockSpec((1, GROUP_W), const),
        ],
        out_specs=pl.BlockSpec((ts, 3 * GROUP_W), lambda b, s: (b * nst + s, 0)),
        out_shape=jax.ShapeDtypeStruct((t, 3 * GROUP_W), BF16),
        scratch_shapes=[
            pltpu.VMEM((HALO + ts, GROUP_W), F32),
            pltpu.VMEM((HALO + ts, GROUP_W), F32),
            pltpu.VMEM((HALO + ts, GROUP_W), F32),
            pltpu.VMEM((HALO + ts, GROUP_W), F32),
            pltpu.VMEM((8 + ts, GROUP_W), F32),
            pltpu.VMEM((HALO + ts, GROUP_W), F32),
        ],
        compiler_params=pltpu.CompilerParams(
            dimension_semantics=("arbitrary", "arbitrary"), vmem_limit_bytes=VMEM_LIMIT),
        name="mixers",
    )(z, z, poolw_bd, pscale, sconvw, cw, cb, lng, lnb)


def _group_rms(x, gsum_ref):
    x2 = x * x
    hi = x2.astype(BF16)
    lo = (x2 - hi.astype(F32)).astype(BF16)
    ssq = (jnp.dot(hi, gsum_ref[...], preferred_element_type=F32)
           + jnp.dot(lo, gsum_ref[...], preferred_element_type=F32))
    return lax.rsqrt(ssq * (1.0 / HEAD_QK) + EPS)


def _attn_kernel(tbl_ref, q_ref, k_ref, v_ref, gq_ref, gk_ref, gsum_ref, lam_ref, subg_ref,
                 o_ref, kn_scr, vt_scr, bias_scr, cfar_scr, qm_scr, m_scr, l_scr, acc_scr, *, tq, lam_init, starts):
    b = pl.program_id(0)
    qi = pl.program_id(1)
    tk = tq
    nw = N_MAPS * tq
    seq = k_ref.shape[0]

    @pl.when((b == 0) & (qi == 0))
    def _():
        kk = lax.broadcasted_iota(jnp.int32, (tk, tq), 0)
        qq = lax.broadcasted_iota(jnp.int32, (tk, tq), 1)
        for h in range(N_HEADS):
            far = tbl_ref[h * REL_BUCKETS + REL_BUCKETS - 1]
            cfar_scr[:, 2 * h * tq:(2 * h + 2) * tq] = jnp.full((1, 2 * tq), far, F32)
            for off in range(2):
                d = qq - kk + off * tk
                val = jnp.full((tk, tq), tbl_ref[h * REL_BUCKETS], F32)
                for bk in range(1, REL_BUCKETS):
                    val = jnp.where(d >= starts[bk], tbl_ref[h * REL_BUCKETS + bk], val)
                if off == 0:
                    val = jnp.where(d >= 0, val, NEG_INF)
                for mp in range(2):
                    bias_scr[off, :, (2 * h + mp) * tq:(2 * h + mp + 1) * tq] = val

    @pl.when(qi == 0)
    def _():
        for c0 in range(0, seq, tk):
            kf = k_ref[c0:c0 + tk, :].astype(F32)
            kn_scr[c0:c0 + tk, :] = (kf * _group_rms(kf, gsum_ref) * gk_ref[...]).astype(BF16)
            vt_scr[:, c0:c0 + tk] = v_ref[c0:c0 + tk, :].astype(F32).T.astype(BF16)

    qf = q_ref[...].astype(F32)
    qt = (qf * _group_rms(qf, gsum_ref) * gq_ref[...]).T
    grp = lax.broadcasted_iota(jnp.int32, (GROUP_W, tq), 0) // HEAD_QK
    for g in range(N_MAPS):
        qm_scr[:, g * tq:(g + 1) * tq] = jnp.where(grp == g, qt, 0.0).astype(BF16)

    m_scr[...] = jnp.full((1, nw), NEG_INF, F32)
    l_scr[...] = jnp.zeros((1, nw), F32)
    acc_scr[...] = jnp.zeros((N_HEADS, HEAD_V, 2 * tq), F32)

    def block(j, bias_tile):
        c0 = pl.multiple_of(j * tk, tk)
        s = jnp.dot(kn_scr[pl.ds(c0, tk), :], qm_scr[...], preferred_element_type=F32)
        m_old = m_scr[...]
        if bias_tile is None:
            c = cfar_scr[...]
            m_new = jnp.maximum(m_old, jnp.max(s, axis=0, keepdims=True) + c)
            p = jnp.exp2(s - (m_new - c))
        else:
            s = s + bias_scr[bias_tile]
            m_new = jnp.maximum(m_old, jnp.max(s, axis=0, keepdims=True))
            p = jnp.exp2(s - m_new)
        alpha = jnp.exp2(m_old - m_new)
        l_scr[...] = alpha * l_scr[...] + jnp.sum(p, axis=0, keepdims=True)
        m_scr[...] = m_new
        pb = p.astype(BF16)
        for h in range(N_HEADS):
            w = slice(2 * h * tq, (2 * h + 2) * tq)
            pv = jnp.dot(vt_scr[h * HEAD_V:(h + 1) * HEAD_V, pl.ds(c0, tk)], pb[:, w],
                         preferred_element_type=F32)
            acc_scr[h] = alpha[:, w] * acc_scr[h] + pv

    def far_body(j, carry):
        block(j, None)
        return carry

    lax.fori_loop(0, qi - 1, far_body, 0)

    @pl.when(qi >= 1)
    def _():
        block(qi - 1, 1)

    block(qi, 0)

    lp = lam_ref[...]
    lam = (jnp.exp(jnp.sum(lp[0:1] * lp[1:2], axis=-1, keepdims=True))
           - jnp.exp(jnp.sum(lp[2:3] * lp[3:4], axis=-1, keepdims=True)) + lam_init)
    heads = []
    for h in range(N_HEADS):
        r = acc_scr[h] / l_scr[:, 2 * h * tq:(2 * h + 2) * tq]
        o_h = r[:, :tq] - lam * r[:, tq:]
        ms = jnp.mean(o_h * o_h, axis=0, keepdims=True)
        heads.append(o_h * lax.rsqrt(ms + EPS) * subg_ref[...] * (1.0 - lam_init))
    o_ref[...] = jnp.concatenate(heads, axis=0).T.astype(o_ref.dtype)


def _attention(z, tbl, gq, gk, gsum, lam_params, subg_col, batch, seq, tq, lam_init):
    t = z.shape[0]
    nq = seq // tq
    starts = _bucket_starts(2 * tq)
    assert starts[REL_BUCKETS - 1] <= tq, "blocks two or more tiles below the diagonal must sit in the last bucket"
    const = lambda b, i: (0, 0)
    kern = functools.partial(_attn_kernel, tq=tq, lam_init=lam_init, starts=starts)
    return pl.pallas_call(
        kern,
        grid=(batch, nq),
        in_specs=[
            pl.BlockSpec(memory_space=pltpu.SMEM),
            pl.BlockSpec((tq, GROUP_W), lambda b, i: (b * nq + i, 6)),
            pl.BlockSpec((seq, GROUP_W), lambda b, i: (b, 7)),
            pl.BlockSpec((seq, GROUP_W), lambda b, i: (b, 8)),
            pl.BlockSpec((1, GROUP_W), const),
            pl.BlockSpec((1, GROUP_W), const),
            pl.BlockSpec((GROUP_W, GROUP_W), const),
            pl.BlockSpec((4, HEAD_QK), const),
            pl.BlockSpec((HEAD_V, 1), const),
        ],
        out_specs=pl.BlockSpec((tq, GROUP_W), lambda b, i: (b * nq + i, 0)),
        out_shape=jax.ShapeDtypeStruct((t, GROUP_W), BF16),
        scratch_shapes=[
            pltpu.VMEM((seq, GROUP_W), BF16),
            pltpu.VMEM((GROUP_W, seq), BF16),
            pltpu.VMEM((2, tq, N_MAPS * tq), F32),
            pltpu.VMEM((1, N_MAPS * tq), F32),
            pltpu.VMEM((GROUP_W, N_MAPS * tq), BF16),
            pltpu.VMEM((1, N_MAPS * tq), F32),
            pltpu.VMEM((1, N_MAPS * tq), F32),
            pltpu.VMEM((N_HEADS, HEAD_V, 2 * tq), F32),
        ],
        compiler_params=pltpu.CompilerParams(
            dimension_semantics=("arbitrary", "arbitrary"), vmem_limit_bytes=VMEM_LIMIT),
        name="diff_attention",
    )(tbl, z, z, z, gq, gk, gsum, lam_params, subg_col)


def _out_ffn_kernel(x_ref, mix_ref, att_ref, wo_ref, g2_ref, wg_ref, wu_ref, wd_ref, o_ref, a_scr, *, ff_chunk):
    mix_w = mix_ref.shape[1]
    x1 = (x_ref[...]
          + jnp.dot(mix_ref[...], wo_ref[0:mix_w, :], preferred_element_type=F32)
          + jnp.dot(att_ref[...], wo_ref[mix_w:, :], preferred_element_type=F32))
    ms = jnp.mean(x1 * x1, axis=-1, keepdims=True)
    h2 = (x1 * lax.rsqrt(ms + EPS) * g2_ref[...]).astype(BF16)
    for c0 in range(0, D_FF, ff_chunk):
        gate = jnp.dot(h2, wg_ref[:, c0:c0 + ff_chunk], preferred_element_type=F32)
        up = jnp.dot(h2, wu_ref[:, c0:c0 + ff_chunk], preferred_element_type=F32)
        a_scr[:, c0:c0 + ff_chunk] = (gate * jax.nn.sigmoid(gate) * up).astype(BF16)
    o_ref[...] = x1 + jnp.dot(a_scr[...], wd_ref[...], preferred_element_type=F32)


def _out_ffn(x2d, mix, att, wo, g2, wg, wu, wd, tm):
    t = x2d.shape[0]
    const = lambda i: (0, 0)
    single = pl.Buffered(1)
    kern = functools.partial(_out_ffn_kernel, ff_chunk=256)
    return pl.pallas_call(
        kern,
        grid=(t // tm,),
        in_specs=[
            pl.BlockSpec((tm, D_MODEL), lambda i: (i, 0)),
            pl.BlockSpec((tm, mix.shape[1]), lambda i: (i, 0)),
            pl.BlockSpec((tm, att.shape[1]), lambda i: (i, 0)),
            pl.BlockSpec((D_MODEL, D_MODEL), const, pipeline_mode=single),
            pl.BlockSpec((1, D_MODEL), const),
            pl.BlockSpec((D_MODEL, D_FF), const, pipeline_mode=single),
            pl.BlockSpec((D_MODEL, D_FF), const, pipeline_mode=single),
            pl.BlockSpec((D_FF, D_MODEL), const, pipeline_mode=single),
        ],
        out_specs=pl.BlockSpec((tm, D_MODEL), lambda i: (i, 0)),
        out_shape=jax.ShapeDtypeStruct((t, D_MODEL), F32),
        scratch_shapes=[pltpu.VMEM((tm, D_FF), BF16)],
        compiler_params=pltpu.CompilerParams(
            dimension_semantics=("arbitrary",), vmem_limit_bytes=VMEM_LIMIT),
        name="out_ffn",
    )(x2d, mix, att, wo, g2, wg, wu, wd)


def _block_diag_pool(pool_w):
    out = jnp.zeros((GROUP_W, GROUP_W), pool_w.dtype)
    for g in range(len(POOL_WINDOWS)):
        out = out.at[g * POOL_GROUP:(g + 1) * POOL_GROUP, g * POOL_GROUP:(g + 1) * POOL_GROUP].set(pool_w[g])
    return out


def kernel(x, norm1_g, w_in, pool_w, pool_scale, sconv_w, conf_dw_w, conf_dw_b, conf_ln_g, conf_ln_b,
           q_norm_g, k_norm_g, lam_q1, lam_k1, lam_q2, lam_k2, subln_g, w_out, norm2_g, w_gate, w_up,
           w_down, rel_bias):
    batch, seq, d = x.shape
    assert d == D_MODEL and w_in.shape[-1] == D_IN
    depth = w_in.shape[0]
    t = batch * seq
    tm = min(512, t)
    ts = min(256, seq)
    tq = min(256, seq)
    assert t % tm == 0 and seq % ts == 0 and seq % tq == 0 and ts % HALO == 0

    gid = np.arange(GROUP_W) // HEAD_QK
    gsum = jnp.asarray((gid[:, None] == gid[None, :]).astype(np.float32), BF16)
    tbl = (rel_bias.astype(F32).T * LOG2E).reshape(-1)
    row = lambda v: v.astype(F32).reshape(1, -1)

    x2d = x.reshape(t, D_MODEL)
    for l in range(depth):
        lam_init = 0.8 - 0.6 * math.exp(-0.3 * l)
        z = _in_proj(x2d, row(norm1_g[l]), w_in[l].astype(BF16), tm)
        mix = _mixers(z, _block_diag_pool(pool_w[l]).astype(BF16), row(pool_scale[l]), sconv_w[l].astype(F32),
                      conf_dw_w[l].astype(F32), row(conf_dw_b[l]), row(conf_ln_g[l]), row(conf_ln_b[l]),
                      batch, seq, ts)
        gq = row(jnp.tile(q_norm_g[l].astype(F32), N_MAPS) * (HEAD_QK ** -0.5 * LOG2E))
        gk = row(jnp.tile(k_norm_g[l].astype(F32), N_MAPS))
        lam_params = jnp.stack([lam_q1[l], lam_k1[l], lam_q2[l], lam_k2[l]]).astype(F32)
        att = _attention(z, tbl, gq, gk, gsum, lam_params, subln_g[l].astype(F32).reshape(-1, 1), batch, seq, tq,
                         lam_init)
        x2d = _out_ffn(x2d, mix, att, w_out[l].astype(BF16), row(norm2_g[l]), w_gate[l].astype(BF16),
                       w_up[l].astype(BF16), w_down[l].astype(BF16), tm)
    return x2d.reshape(batch, seq, D_MODEL)
```

```python
import functools
import math

import numpy as np
import jax
import jax.numpy as jnp
from jax import lax
from jax.experimental import pallas as pl
from jax.experimental.pallas import tpu as pltpu

F32 = jnp.float32
BF16 = jnp.bfloat16

D_MODEL = 1024
GROUP_W = 256
N_SLICES = 9
D_IN = N_SLICES * GROUP_W
N_MIX_SLICES = 6
POOL_WINDOWS = (2, 4, 8, 16)
POOL_GROUP = GROUP_W // len(POOL_WINDOWS)
SCONV_K = 3
CONF_K = 31
N_HEADS = 4
HEAD_V = GROUP_W // N_HEADS
HEAD_QK = HEAD_V // 2
N_MAPS = 2 * N_HEADS
D_FF = 2816
REL_BUCKETS = 32
REL_MAX_EXACT = 16
REL_MAX_DIST = 128
EPS = 1e-6
NEG_INF = -1e30
LOG2E = math.log2(math.e)

HALO = 32
VMEM_LIMIT = 56 * 1024 * 1024


def _bucket_starts(max_dist):
    n = np.arange(0, max_dist + 1)
    nf = np.maximum(n, 1).astype(np.float32)
    large = REL_MAX_EXACT + (np.log(nf / np.float32(REL_MAX_EXACT)) / np.float32(math.log(REL_MAX_DIST / REL_MAX_EXACT))
                             * np.float32(REL_BUCKETS - REL_MAX_EXACT)).astype(np.int32)
    large = np.minimum(large, REL_BUCKETS - 1)
    bucket = np.where(n < REL_MAX_EXACT, n, large)
    starts = [int(np.argmax(bucket >= b)) for b in range(REL_BUCKETS)]
    assert bucket[-1] == REL_BUCKETS - 1
    return starts


def _in_proj_kernel(x_ref, g_ref, w_ref, z_ref):
    x = x_ref[...]
    ms = jnp.mean(x * x, axis=-1, keepdims=True)
    h = x * lax.rsqrt(ms + EPS) * g_ref[...]
    z = jnp.dot(h.astype(BF16), w_ref[...], preferred_element_type=F32)
    z_ref[...] = z.astype(z_ref.dtype)


def _in_proj(x2d, g, w, tm):
    t = x2d.shape[0]
    return pl.pallas_call(
        _in_proj_kernel,
        grid=(t // tm,),
        in_specs=[
            pl.BlockSpec((tm, D_MODEL), lambda i: (i, 0)),
            pl.BlockSpec((1, D_MODEL), lambda i: (0, 0)),
            pl.BlockSpec((D_MODEL, D_IN), lambda i: (0, 0)),
        ],
        out_specs=pl.BlockSpec((tm, D_IN), lambda i: (i, 0)),
        out_shape=jax.ShapeDtypeStruct((t, D_IN), BF16),
        compiler_params=pltpu.CompilerParams(
            dimension_semantics=("arbitrary",), vmem_limit_bytes=VMEM_LIMIT),
        name="in_proj",
    )(x2d, g, w)


def _mixer_kernel(zc_ref, zh_ref, poolw_ref, pscale_ref, sconvw_ref, cw_ref, cb_ref, lng_ref, lnb_ref,
                  out_ref, a0, b1, b2, b3, p_ext, u_ext, *, ts, row_chunk):
    first = pl.program_id(1) == 0
    keep = jnp.where(first, 0.0, 1.0).astype(F32)
    n = HALO + ts
    W = GROUP_W

    def cur(k):
        return zc_ref[:, k * W:(k + 1) * W].astype(F32)

    def halo(k):
        return zh_ref[:, k * W:(k + 1) * W].astype(F32) * keep

    zp = cur(0)
    a0[0:HALO, :] = halo(0)
    a0[HALO:n, :] = zp
    b1[8:n, :] = a0[8:n, :] + a0[7:n - 1, :]
    b2[16:n, :] = b1[16:n, :] + b1[14:n - 2, :]
    b3[24:n, :] = b2[24:n, :] + b2[20:n - 4, :]
    s16 = b3[HALO:n, :] + b3[HALO - 8:n - 8, :]
    lane = lax.broadcasted_iota(jnp.int32, (ts, W), 1)
    pos1 = pl.program_id(1) * ts + lax.broadcasted_iota(jnp.int32, (ts, W), 0) + 1
    sel = jnp.where(lane < POOL_GROUP, b1[HALO:n, :],
                    jnp.where(lane < 2 * POOL_GROUP, b2[HALO:n, :],
                              jnp.where(lane < 3 * POOL_GROUP, b3[HALO:n, :], s16)))
    win = jnp.where(lane < POOL_GROUP, POOL_WINDOWS[0],
                    jnp.where(lane < 2 * POOL_GROUP, POOL_WINDOWS[1],
                              jnp.where(lane < 3 * POOL_GROUP, POOL_WINDOWS[2], POOL_WINDOWS[3])))
    cnt = jnp.minimum(pos1, win).astype(F32)
    pooled = sel / cnt - zp
    y_pool = jnp.dot(pooled.astype(BF16), poolw_ref[...], preferred_element_type=F32) * pscale_ref[...]
    out_ref[:, 0:W] = y_pool.astype(out_ref.dtype)

    p_ext[0:8, :] = (zh_ref[HALO - 8:HALO, 2 * W:3 * W].astype(F32)
                     * zh_ref[HALO - 8:HALO, 3 * W:4 * W].astype(F32)) * keep
    p_ext[8:8 + ts, :] = cur(2) * cur(3)
    conv3 = sconvw_ref[SCONV_K - 1:SCONV_K, :] * p_ext[8:8 + ts, :]
    for j in range(1, SCONV_K):
        conv3 = conv3 + sconvw_ref[SCONV_K - 1 - j:SCONV_K - j, :] * p_ext[8 - j:8 - j + ts, :]
    out_ref[:, W:2 * W] = (cur(1) * conv3).astype(out_ref.dtype)

    u_ext[0:HALO, :] = halo(4) * jax.nn.sigmoid(zh_ref[:, 5 * W:6 * W].astype(F32))
    u_ext[HALO:n, :] = cur(4) * jax.nn.sigmoid(cur(5))
    for r0 in range(0, ts, row_chunk):
        acc = cb_ref[...] + cw_ref[CONF_K - 1:CONF_K, :] * u_ext[HALO + r0:HALO + r0 + row_chunk, :]
        for j in range(1, CONF_K):
            acc = acc + cw_ref[CONF_K - 1 - j:CONF_K - j, :] * u_ext[HALO + r0 - j:HALO + r0 - j + row_chunk, :]
        mu = jnp.mean(acc, axis=-1, keepdims=True)
        xc = acc - mu
        var = jnp.mean(xc * xc, axis=-1, keepdims=True)
        yn = xc * lax.rsqrt(var + EPS) * lng_ref[...] + lnb_ref[...]
        out_ref[r0:r0 + row_chunk, 2 * W:3 * W] = (yn * jax.nn.sigmoid(yn)).astype(out_ref.dtype)


def _mixers(z, poolw_bd, pscale, sconvw, cw, cb, lng, lnb, batch, seq, ts):
    t = z.shape[0]
    nst = seq // ts
    hb = ts // HALO
    mix_w = N_MIX_SLICES * GROUP_W
    const = lambda b, s: (0, 0)
    kern = functools.partial(_mixer_kernel, ts=ts, row_chunk=64)
    return pl.pallas_call(
        kern,
        grid=(batch, nst),
        in_specs=[
            pl.BlockSpec((ts, mix_w), lambda b, s: (b * nst + s, 0)),
            pl.BlockSpec((HALO, mix_w), lambda b, s: (jnp.maximum((b * nst + s) * hb - 1, 0), 0)),
            pl.BlockSpec((GROUP_W, GROUP_W), const),
            pl.BlockSpec((1, GROUP_W), const),
            pl.BlockSpec((SCONV_K, GROUP_W), const),
            pl.BlockSpec((CONF_K, GROUP_W), const),
            pl.BlockSpec((1, GROUP_W), const),
            pl.BlockSpec((1, GROUP_W), const),
            pl.BlockSpec((1, GROUP_W), const),
        ],
        out_specs=pl.BlockSpec((ts, 3 * GROUP_W), lambda b, s: (b * nst + s, 0)),
        out_shape=jax.ShapeDtypeStruct((t, 3 * GROUP_W), BF16),
        scratch_shapes=[
            pltpu.VMEM((HALO + ts, GROUP_W), F32),
            pltpu.VMEM((HALO + ts, GROUP_W), F32),
            pltpu.VMEM((HALO + ts, GROUP_W), F32),
            pltpu.VMEM((HALO + ts, GROUP_W), F32),
            pltpu.VMEM((8 + ts, GROUP_W), F32),
            pltpu.VMEM((HALO + ts, GROUP_W), F32),
        ],
        compiler_params=pltpu.CompilerParams(
            dimension_semantics=("arbitrary", "arbitrary"), vmem_limit_bytes=VMEM_LIMIT),
        name="mixers",
    )(z, z, poolw_bd, pscale, sconvw, cw, cb, lng, lnb)


def _group_rms(x, gsum_ref):
    x2 = x * x
    hi = x2.astype(BF16)
    lo = (x2 - hi.astype(F32)).astype(BF16)
    ssq = (jnp.dot(hi, gsum_ref[...], preferred_element_type=F32)
           + jnp.dot(lo, gsum_ref[...], preferred_element_type=F32))
    return lax.rsqrt(ssq * (1.0 / HEAD_QK) + EPS)


def _attn_kernel(tbl_ref, qa_ref, qb_ref, k_ref, v_ref, gq_ref, gk_ref, gsum_ref, lam_ref, subg_ref,
                 o_ref, kn_scr, vt_scr, bias_scr, cfar_scr, qm_scr, m_scr, l_scr, acc_scr, s0_scr, s1_scr,
                 *, tq, nq, lam_init, starts):
    b = pl.program_id(0)
    t = pl.program_id(1)
    tk = tq
    nw = N_MAPS * tq
    seq = k_ref.shape[0]
    DIAG, SUB, FAR = 0, 1, 2

    @pl.when((b == 0) & (t == 0))
    def _():
        kk = lax.broadcasted_iota(jnp.int32, (tk, tq), 0)
        qq = lax.broadcasted_iota(jnp.int32, (tk, tq), 1)
        for h in range(N_HEADS):
            far = tbl_ref[h * REL_BUCKETS + REL_BUCKETS - 1]
            cfar_scr[:, 2 * h * tq:(2 * h + 2) * tq] = jnp.full((1, 2 * tq), far, F32)
            bias_scr[FAR, :, 2 * h * tq:(2 * h + 2) * tq] = jnp.full((tk, 2 * tq), far, F32)
            for off in (DIAG, SUB):
                d = qq - kk + off * tk
                val = jnp.full((tk, tq), tbl_ref[h * REL_BUCKETS], F32)
                for bk in range(1, REL_BUCKETS):
                    val = jnp.where(d >= starts[bk], tbl_ref[h * REL_BUCKETS + bk], val)
                if off == DIAG:
                    val = jnp.where(d >= 0, val, NEG_INF)
                for mp in range(2):
                    bias_scr[off, :, (2 * h + mp) * tq:(2 * h + mp + 1) * tq] = val

    @pl.when(t == 0)
    def _():
        for c0 in range(0, seq, tk):
            kf = k_ref[c0:c0 + tk, :].astype(F32)
            kn_scr[c0:c0 + tk, :] = (kf * _group_rms(kf, gsum_ref) * gk_ref[...]).astype(BF16)
            vt_scr[:, c0:c0 + tk] = v_ref[c0:c0 + tk, :].astype(F32).T.astype(BF16)

    grp = lax.broadcasted_iota(jnp.int32, (GROUP_W, tq), 0) // HEAD_QK
    for x, q_ref in enumerate((qa_ref, qb_ref)):
        qf = q_ref[...].astype(F32)
        qt = (qf * _group_rms(qf, gsum_ref) * gq_ref[...]).T
        for g in range(N_MAPS):
            qm_scr[x, :, g * tq:(g + 1) * tq] = jnp.where(grp == g, qt, 0.0).astype(BF16)
    m_scr[...] = jnp.full(m_scr.shape, NEG_INF, F32)
    l_scr[...] = jnp.zeros(l_scr.shape, F32)
    acc_scr[...] = jnp.zeros(acc_scr.shape, F32)

    def scores(n, x, j, bias_tile, constant_bias):
        sbuf = (s0_scr, s1_scr)[n % 2].at[jnp.minimum(t, 0)]
        c0 = pl.multiple_of(j * tk, tk)
        s = jnp.dot(kn_scr[pl.ds(c0, tk), :], qm_scr[x], preferred_element_type=F32)
        m_old = m_scr[x]
        if constant_bias:
            sbuf[...] = s
            c = cfar_scr[...]
            m_new = jnp.maximum(m_old, jnp.max(s, axis=0, keepdims=True) + c)
            shift = m_new - c
        else:
            s = s + bias_scr[bias_tile]
            sbuf[...] = s
            m_new = jnp.maximum(m_old, jnp.max(s, axis=0, keepdims=True))
            shift = m_new
        m_scr[x] = m_new
        return x, c0, sbuf, shift, jnp.exp2(m_old - m_new)

    def accumulate(x, c0, sbuf, shift, alpha):
        p = jnp.exp2(sbuf[...] - shift)
        l_scr[x] = alpha * l_scr[x] + jnp.sum(p, axis=0, keepdims=True)
        pb = p.astype(BF16)
        for h in range(N_HEADS):
            w = slice(2 * h * tq, (2 * h + 2) * tq)
            pv = jnp.dot(vt_scr[h * HEAD_V:(h + 1) * HEAD_V, pl.ds(c0, tk)], pb[:, w],
                         preferred_element_type=F32)
            acc_scr[x, h] = alpha[:, w] * acc_scr[x, h] + pv

    qa = t
    qb = nq - 1 - t
    has_sub = t > 0
    n_far_a = jnp.maximum(t - 1, 0)
    blocks = [(0, qa, DIAG, False), (1, qb, DIAG, False), (1, qb - 1, SUB, False),
              (jnp.where(has_sub, 0, 1), jnp.where(has_sub, qa - 1, nq - 3), jnp.where(has_sub, SUB, FAR), False)]
    for i in range(nq - 3):
        to_a = i < n_far_a
        blocks.append((jnp.where(to_a, 0, 1), jnp.where(to_a, i, i - n_far_a), FAR, True))
    staged = scores(0, *blocks[0])
    for n in range(len(blocks)):
        following = scores(n + 1, *blocks[n + 1]) if n + 1 < len(blocks) else None
        accumulate(*staged)
        staged = following

    lp = lam_ref[...]
    lam = (jnp.exp(jnp.sum(lp[0:1] * lp[1:2], axis=-1, keepdims=True))
           - jnp.exp(jnp.sum(lp[2:3] * lp[3:4], axis=-1, keepdims=True)) + lam_init)
    for x in range(2):
        heads = []
        for h in range(N_HEADS):
            r = acc_scr[x, h] / l_scr[x, :, 2 * h * tq:(2 * h + 2) * tq]
            o_h = r[:, :tq] - lam * r[:, tq:]
            ms = jnp.mean(o_h * o_h, axis=0, keepdims=True)
            heads.append(o_h * lax.rsqrt(ms + EPS) * subg_ref[...] * (1.0 - lam_init))
        o_ref[x] = jnp.concatenate(heads, axis=0).T.astype(o_ref.dtype)


def _attention(z, tbl, gq, gk, gsum, lam_params, subg_col, batch, seq, tq, lam_init):
    t = z.shape[0]
    nq = seq // tq
    assert nq % 2 == 0 and nq >= 4
    nh = nq // 2
    starts = _bucket_starts(2 * tq)
    assert starts[REL_BUCKETS - 1] <= tq, "blocks two or more tiles below the diagonal must sit in the last bucket"
    const = lambda b, i: (0, 0)
    kern = functools.partial(_attn_kernel, tq=tq, nq=nq, lam_init=lam_init, starts=starts)
    return pl.pallas_call(
        kern,
        grid=(batch, nh),
        in_specs=[
            pl.BlockSpec(memory_space=pltpu.SMEM),
            pl.BlockSpec((tq, GROUP_W), lambda b, i: (b * nq + i, 6)),
            pl.BlockSpec((tq, GROUP_W), lambda b, i: (b * nq + nq - 1 - i, 6)),
            pl.BlockSpec((seq, GROUP_W), lambda b, i: (b, 7)),
            pl.BlockSpec((seq, GROUP_W), lambda b, i: (b, 8)),
            pl.BlockSpec((1, GROUP_W), const),
            pl.BlockSpec((1, GROUP_W), const),
            pl.BlockSpec((GROUP_W, GROUP_W), const),
            pl.BlockSpec((4, HEAD_QK), const),
            pl.BlockSpec((HEAD_V, 1), const),
        ],
        out_specs=pl.BlockSpec((2, tq, GROUP_W), lambda b, i: (0, b * nh + i, 0)),
        out_shape=jax.ShapeDtypeStruct((2, t // 2, GROUP_W), BF16),
        scratch_shapes=[
            pltpu.VMEM((seq, GROUP_W), BF16),
            pltpu.VMEM((GROUP_W, seq), BF16),
            pltpu.VMEM((3, tq, N_MAPS * tq), F32),
            pltpu.VMEM((1, N_MAPS * tq), F32),
            pltpu.VMEM((2, GROUP_W, N_MAPS * tq), BF16),
            pltpu.VMEM((2, 1, N_MAPS * tq), F32),
            pltpu.VMEM((2, 1, N_MAPS * tq), F32),
            pltpu.VMEM((2, N_HEADS, HEAD_V, 2 * tq), F32),
            pltpu.VMEM((1, tq, N_MAPS * tq), F32),
            pltpu.VMEM((1, tq, N_MAPS * tq), F32),
        ],
        compiler_params=pltpu.CompilerParams(
            dimension_semantics=("arbitrary", "arbitrary"), vmem_limit_bytes=VMEM_LIMIT),
        name="diff_attention",
    )(tbl, z, z, z, z, gq, gk, gsum, lam_params, subg_col)


def _out_ffn_kernel(x_ref, mix_ref, att0_ref, att1_ref, wo_ref, g2_ref, wg_ref, wu_ref, wd_ref, o_ref, a_scr,
                    *, ff_chunk):
    mix_w = mix_ref.shape[1]
    att = jnp.concatenate([att0_ref[...], att1_ref[...]], axis=0)
    x1 = (x_ref[...]
          + jnp.dot(mix_ref[...], wo_ref[0:mix_w, :], preferred_element_type=F32)
          + jnp.dot(att, wo_ref[mix_w:, :], preferred_element_type=F32))
    ms = jnp.mean(x1 * x1, axis=-1, keepdims=True)
    h2 = (x1 * lax.rsqrt(ms + EPS) * g2_ref[...]).astype(BF16)
    for c0 in range(0, D_FF, ff_chunk):
        gate = jnp.dot(h2, wg_ref[:, c0:c0 + ff_chunk], preferred_element_type=F32)
        up = jnp.dot(h2, wu_ref[:, c0:c0 + ff_chunk], preferred_element_type=F32)
        a_scr[:, c0:c0 + ff_chunk] = (gate * jax.nn.sigmoid(gate) * up).astype(BF16)
    o_ref[...] = x1 + jnp.dot(a_scr[...], wd_ref[...], preferred_element_type=F32)


def _out_ffn(x2d, mix, att, wo, g2, wg, wu, wd, tm, seq, tq):
    t = x2d.shape[0]
    nq = seq // tq
    nh = nq // 2
    assert tm == 2 * tq
    tiles_per_seq = seq // tm
    const = lambda i: (0, 0)
    single = pl.Buffered(1)

    def att_block(which):
        def index(i):
            bb = i // tiles_per_seq
            u = 2 * (i % tiles_per_seq) + which
            lower = u < nh
            return (jnp.where(lower, 0, 1), bb * nh + jnp.where(lower, u, nq - 1 - u), 0)
        return pl.BlockSpec((None, tq, GROUP_W), index)

    kern = functools.partial(_out_ffn_kernel, ff_chunk=256)
    return pl.pallas_call(
        kern,
        grid=(t // tm,),
        in_specs=[
            pl.BlockSpec((tm, D_MODEL), lambda i: (i, 0)),
            pl.BlockSpec((tm, mix.shape[1]), lambda i: (i, 0)),
            att_block(0),
            att_block(1),
            pl.BlockSpec((D_MODEL, D_MODEL), const, pipeline_mode=single),
            pl.BlockSpec((1, D_MODEL), const),
            pl.BlockSpec((D_MODEL, D_FF), const, pipeline_mode=single),
            pl.BlockSpec((D_MODEL, D_FF), const, pipeline_mode=single),
            pl.BlockSpec((D_FF, D_MODEL), const, pipeline_mode=single),
        ],
        out_specs=pl.BlockSpec((tm, D_MODEL), lambda i: (i, 0)),
        out_shape=jax.ShapeDtypeStruct((t, D_MODEL), F32),
        scratch_shapes=[pltpu.VMEM((tm, D_FF), BF16)],
        compiler_params=pltpu.CompilerParams(
            dimension_semantics=("arbitrary",), vmem_limit_bytes=VMEM_LIMIT),
        name="out_ffn",
    )(x2d, mix, att, att, wo, g2, wg, wu, wd)


def _block_diag_pool(pool_w):
    out = jnp.zeros((GROUP_W, GROUP_W), pool_w.dtype)
    for g in range(len(POOL_WINDOWS)):
        out = out.at[g * POOL_GROUP:(g + 1) * POOL_GROUP, g * POOL_GROUP:(g + 1) * POOL_GROUP].set(pool_w[g])
    return out


def kernel(x, norm1_g, w_in, pool_w, pool_scale, sconv_w, conf_dw_w, conf_dw_b, conf_ln_g, conf_ln_b,
           q_norm_g, k_norm_g, lam_q1, lam_k1, lam_q2, lam_k2, subln_g, w_out, norm2_g, w_gate, w_up,
           w_down, rel_bias):
    batch, seq, d = x.shape
    assert d == D_MODEL and w_in.shape[-1] == D_IN
    depth = w_in.shape[0]
    t = batch * seq
    tq = 256
    tm = 2 * tq
    ts = 256
    assert seq % tm == 0 and seq % ts == 0 and ts % HALO == 0

    gid = np.arange(GROUP_W) // HEAD_QK
    gsum = jnp.asarray((gid[:, None] == gid[None, :]).astype(np.float32), BF16)
    tbl = (rel_bias.astype(F32).T * LOG2E).reshape(-1)
    row = lambda v: v.astype(F32).reshape(1, -1)

    x2d = x.reshape(t, D_MODEL)
    for l in range(depth):
        lam_init = 0.8 - 0.6 * math.exp(-0.3 * l)
        z = _in_proj(x2d, row(norm1_g[l]), w_in[l].astype(BF16), tm)
        mix = _mixers(z, _block_diag_pool(pool_w[l]).astype(BF16), row(pool_scale[l]), sconv_w[l].astype(F32),
                      conf_dw_w[l].astype(F32), row(conf_dw_b[l]), row(conf_ln_g[l]), row(conf_ln_b[l]),
                      batch, seq, ts)
        gq = row(jnp.tile(q_norm_g[l].astype(F32), N_MAPS) * (HEAD_QK ** -0.5 * LOG2E))
        gk = row(jnp.tile(k_norm_g[l].astype(F32), N_MAPS))
        lam_params = jnp.stack([lam_q1[l], lam_k1[l], lam_q2[l], lam_k2[l]]).astype(F32)
        att = _attention(z, tbl, gq, gk, gsum, lam_params, subln_g[l].astype(F32).reshape(-1, 1), batch, seq, tq,
                         lam_init)
        x2d = _out_ffn(x2d, mix, att, w_out[l].astype(BF16), row(norm2_g[l]), w_gate[l].astype(BF16),
                       w_up[l].astype(BF16), w_down[l].astype(BF16), tm, seq, tq)
    return x2d.reshape(batch, seq, D_MODEL)
```

```python
import functools
import math

import numpy as np
import jax
import jax.numpy as jnp
from jax import lax
from jax.experimental import pallas as pl
from jax.experimental.pallas import tpu as pltpu

F32 = jnp.float32
BF16 = jnp.bfloat16

D_MODEL = 1024
GROUP_W = 256
N_SLICES = 9
D_IN = N_SLICES * GROUP_W
N_MIX_SLICES = 6
POOL_WINDOWS = (2, 4, 8, 16)
POOL_GROUP = GROUP_W // len(POOL_WINDOWS)
SCONV_K = 3
CONF_K = 31
N_HEADS = 4
HEAD_V = GROUP_W // N_HEADS
HEAD_QK = HEAD_V // 2
N_MAPS = 2 * N_HEADS
D_FF = 2816
REL_BUCKETS = 32
REL_MAX_EXACT = 16
REL_MAX_DIST = 128
EPS = 1e-6
NEG_INF = -1e30
LOG2E = math.log2(math.e)

HALO = 32
VMEM_LIMIT = 56 * 1024 * 1024


def _bucket_starts(max_dist):
    n = np.arange(0, max_dist + 1)
    nf = np.maximum(n, 1).astype(np.float32)
    large = REL_MAX_EXACT + (np.log(nf / np.float32(REL_MAX_EXACT)) / np.float32(math.log(REL_MAX_DIST / REL_MAX_EXACT))
                             * np.float32(REL_BUCKETS - REL_MAX_EXACT)).astype(np.int32)
    large = np.minimum(large, REL_BUCKETS - 1)
    bucket = np.where(n < REL_MAX_EXACT, n, large)
    starts = [int(np.argmax(bucket >= b)) for b in range(REL_BUCKETS)]
    assert bucket[-1] == REL_BUCKETS - 1
    return starts


def _in_mix_kernel(x_ref, g1_ref, w_ref, poolw_ref, pscale_ref, sconvw_ref, cw_ref, cb_ref, lng_ref, lnb_ref,
                   qkv_ref, mix_ref, a0, b1, b2, b3, p_ext, u_ext, u_sh, zb, *, ts, row_chunk):
    n = HALO + ts
    W = GROUP_W

    @pl.when(pl.program_id(1) == 0)
    def _():
        a0[0:HALO, :] = jnp.zeros((HALO, W), F32)
        p_ext[0:8, :] = jnp.zeros((8, W), F32)
        u_ext[0:HALO, :] = jnp.zeros((HALO, W), F32)

    x = x_ref[...]
    ms = jnp.mean(x * x, axis=-1, keepdims=True)
    h = (x * lax.rsqrt(ms + EPS) * g1_ref[...]).astype(BF16)

    def z(k):
        return jnp.dot(h, w_ref[:, k * W:(k + 1) * W], preferred_element_type=F32)

    u_ext[HALO:n, :] = z(4) * jax.nn.sigmoid(z(5))
    for r in range(1, 8):
        u_sh[r, 8:n, :] = u_ext[8 - r:n - r, :]

    def project_qkv(i):
        def job():
            qkv_ref[:, i * W:(i + 1) * W] = z(N_MIX_SLICES + i).astype(qkv_ref.dtype)
        return job

    def project_pool():
        a0[HALO:n, :] = z(0)

    def project_gate():
        zb[...] = z(1)

    def project_conv3_in():
        p_ext[8:8 + ts, :] = z(2) * z(3)

    jobs = [project_qkv(0), project_qkv(1), project_qkv(2), project_pool, project_conv3_in, project_gate]
    chunks = list(range(0, ts, row_chunk))
    assert len(chunks) >= len(jobs)
    for ci, r0 in enumerate(chunks):
        if ci < len(jobs):
            jobs[ci]()
        acc = cb_ref[...] + cw_ref[CONF_K - 1:CONF_K, :] * u_ext[HALO + r0:HALO + r0 + row_chunk, :]
        for j in range(1, CONF_K):
            q8, r = 8 * (j // 8), j % 8
            lo = HALO + r0 - q8
            src = u_ext[lo:lo + row_chunk, :] if r == 0 else u_sh[r, lo:lo + row_chunk, :]
            acc = acc + cw_ref[CONF_K - 1 - j:CONF_K - j, :] * src
        mu = jnp.mean(acc, axis=-1, keepdims=True)
        xc = acc - mu
        var = jnp.mean(xc * xc, axis=-1, keepdims=True)
        yn = xc * lax.rsqrt(var + EPS) * lng_ref[...] + lnb_ref[...]
        mix_ref[r0:r0 + row_chunk, 2 * W:3 * W] = (yn * jax.nn.sigmoid(yn)).astype(mix_ref.dtype)
    u_ext[0:HALO, :] = u_ext[ts:n, :]

    b1[8:n, :] = a0[8:n, :] + a0[7:n - 1, :]
    b2[16:n, :] = b1[16:n, :] + b1[14:n - 2, :]
    b3[24:n, :] = b2[24:n, :] + b2[20:n - 4, :]
    s16 = b3[HALO:n, :] + b3[HALO - 8:n - 8, :]
    lane = lax.broadcasted_iota(jnp.int32, (ts, W), 1)
    pos1 = pl.program_id(1) * ts + lax.broadcasted_iota(jnp.int32, (ts, W), 0) + 1
    sel = jnp.where(lane < POOL_GROUP, b1[HALO:n, :],
                    jnp.where(lane < 2 * POOL_GROUP, b2[HALO:n, :],
                              jnp.where(lane < 3 * POOL_GROUP, b3[HALO:n, :], s16)))
    win = jnp.where(lane < POOL_GROUP, POOL_WINDOWS[0],
                    jnp.where(lane < 2 * POOL_GROUP, POOL_WINDOWS[1],
                              jnp.where(lane < 3 * POOL_GROUP, POOL_WINDOWS[2], POOL_WINDOWS[3])))
    cnt = jnp.minimum(pos1, win).astype(F32)
    pooled = sel / cnt - a0[HALO:n, :]
    y_pool = jnp.dot(pooled.astype(BF16), poolw_ref[...], preferred_element_type=F32) * pscale_ref[...]
    mix_ref[:, 0:W] = y_pool.astype(mix_ref.dtype)
    a0[0:HALO, :] = a0[ts:n, :]

    conv3 = sconvw_ref[SCONV_K - 1:SCONV_K, :] * p_ext[8:8 + ts, :]
    for j in range(1, SCONV_K):
        conv3 = conv3 + sconvw_ref[SCONV_K - 1 - j:SCONV_K - j, :] * p_ext[8 - j:8 - j + ts, :]
    mix_ref[:, W:2 * W] = (zb[...] * conv3).astype(mix_ref.dtype)
    p_ext[0:8, :] = p_ext[ts:ts + 8, :]


def _in_mix(x2d, g1, w, poolw_bd, pscale, sconvw, cw, cb, lng, lnb, batch, seq, ts):
    t = x2d.shape[0]
    nst = seq // ts
    const = lambda b, s: (0, 0)
    tile = lambda b, s: (b * nst + s, 0)
    kern = functools.partial(_in_mix_kernel, ts=ts, row_chunk=32)
    out_w = 3 * GROUP_W
    return pl.pallas_call(
        kern,
        grid=(batch, nst),
        in_specs=[
            pl.BlockSpec((ts, D_MODEL), tile),
            pl.BlockSpec((1, D_MODEL), const),
            pl.BlockSpec((D_MODEL, D_IN), const, pipeline_mode=pl.Buffered(1)),
            pl.BlockSpec((GROUP_W, GROUP_W), const),
            pl.BlockSpec((1, GROUP_W), const),
            pl.BlockSpec((SCONV_K, GROUP_W), const),
            pl.BlockSpec((CONF_K, GROUP_W), const),
            pl.BlockSpec((1, GROUP_W), const),
            pl.BlockSpec((1, GROUP_W), const),
            pl.BlockSpec((1, GROUP_W), const),
        ],
        out_specs=[pl.BlockSpec((ts, out_w), tile), pl.BlockSpec((ts, out_w), tile)],
        out_shape=[jax.ShapeDtypeStruct((t, out_w), BF16), jax.ShapeDtypeStruct((t, out_w), BF16)],
        scratch_shapes=[
            pltpu.VMEM((HALO + ts, GROUP_W), F32),
            pltpu.VMEM((HALO + ts, GROUP_W), F32),
            pltpu.VMEM((HALO + ts, GROUP_W), F32),
            pltpu.VMEM((HALO + ts, GROUP_W), F32),
            pltpu.VMEM((8 + ts, GROUP_W), F32),
            pltpu.VMEM((HALO + ts, GROUP_W), F32),
            pltpu.VMEM((8, HALO + ts, GROUP_W), F32),
            pltpu.VMEM((ts, GROUP_W), F32),
        ],
        compiler_params=pltpu.CompilerParams(
            dimension_semantics=("arbitrary", "arbitrary"), vmem_limit_bytes=VMEM_LIMIT),
        name="in_mix",
    )(x2d, g1, w, poolw_bd, pscale, sconvw, cw, cb, lng, lnb)


def _group_rms(x, gsum_ref):
    x2 = x * x
    hi = x2.astype(BF16)
    lo = (x2 - hi.astype(F32)).astype(BF16)
    ssq = (jnp.dot(hi, gsum_ref[...], preferred_element_type=F32)
           + jnp.dot(lo, gsum_ref[...], preferred_element_type=F32))
    return lax.rsqrt(ssq * (1.0 / HEAD_QK) + EPS)


def _attn_kernel(tbl_ref, qa_ref, qb_ref, k_ref, v_ref, gq_ref, gk_ref, gsum_ref, lam_ref, subg_ref,
                 o_ref, kn_scr, vt_scr, bias_scr, cfar_scr, qm_scr, m_scr, l_scr, acc_scr, s0_scr, s1_scr,
                 *, tq, nq, lam_init, starts):
    b = pl.program_id(0)
    t = pl.program_id(1)
    tk = tq
    nw = N_MAPS * tq
    seq = k_ref.shape[0]
    DIAG, SUB, FAR = 0, 1, 2

    @pl.when((b == 0) & (t == 0))
    def _():
        kk = lax.broadcasted_iota(jnp.int32, (tk, tq), 0)
        qq = lax.broadcasted_iota(jnp.int32, (tk, tq), 1)
        for h in range(N_HEADS):
            far = tbl_ref[h * REL_BUCKETS + REL_BUCKETS - 1]
            cfar_scr[:, 2 * h * tq:(2 * h + 2) * tq] = jnp.full((1, 2 * tq), far, F32)
            bias_scr[FAR, :, 2 * h * tq:(2 * h + 2) * tq] = jnp.full((tk, 2 * tq), far, F32)
            for off in (DIAG, SUB):
                d = qq - kk + off * tk
                val = jnp.full((tk, tq), tbl_ref[h * REL_BUCKETS], F32)
                for bk in range(1, REL_BUCKETS):
                    val = jnp.where(d >= starts[bk], tbl_ref[h * REL_BUCKETS + bk], val)
                if off == DIAG:
                    val = jnp.where(d >= 0, val, NEG_INF)
                for mp in range(2):
                    bias_scr[off, :, (2 * h + mp) * tq:(2 * h + mp + 1) * tq] = val

    @pl.when(t == 0)
    def _():
        for c0 in range(0, seq, tk):
            kf = k_ref[c0:c0 + tk, :].astype(F32)
            kn_scr[c0:c0 + tk, :] = (kf * _group_rms(kf, gsum_ref) * gk_ref[...]).astype(BF16)
            vt_scr[:, c0:c0 + tk] = v_ref[c0:c0 + tk, :].astype(F32).T.astype(BF16)

    grp = lax.broadcasted_iota(jnp.int32, (GROUP_W, tq), 0) // HEAD_QK
    for x, q_ref in enumerate((qa_ref, qb_ref)):
        qf = q_ref[...].astype(F32)
        qt = (qf * _group_rms(qf, gsum_ref) * gq_ref[...]).T
        for g in range(N_MAPS):
            qm_scr[x, :, g * tq:(g + 1) * tq] = jnp.where(grp == g, qt, 0.0).astype(BF16)
    m_scr[...] = jnp.full(m_scr.shape, NEG_INF, F32)
    l_scr[...] = jnp.zeros(l_scr.shape, F32)
    acc_scr[...] = jnp.zeros(acc_scr.shape, F32)

    def scores(n, x, j, bias_tile, constant_bias):
        sbuf = (s0_scr, s1_scr)[n % 2].at[jnp.minimum(t, 0)]
        c0 = pl.multiple_of(j * tk, tk)
        s = jnp.dot(kn_scr[pl.ds(c0, tk), :], qm_scr[x], preferred_element_type=F32)
        m_old = m_scr[x]
        if constant_bias:
            sbuf[...] = s
            c = cfar_scr[...]
            m_new = jnp.maximum(m_old, jnp.max(s, axis=0, keepdims=True) + c)
            shift = m_new - c
        else:
            s = s + bias_scr[bias_tile]
            sbuf[...] = s
            m_new = jnp.maximum(m_old, jnp.max(s, axis=0, keepdims=True))
            shift = m_new
        m_scr[x] = m_new
        return x, c0, sbuf, shift, jnp.exp2(m_old - m_new)

    def accumulate(x, c0, sbuf, shift, alpha):
        p = jnp.exp2(sbuf[...] - shift)
        l_scr[x] = alpha * l_scr[x] + jnp.sum(p, axis=0, keepdims=True)
        pb = p.astype(BF16)
        for h in range(N_HEADS):
            w = slice(2 * h * tq, (2 * h + 2) * tq)
            pv = jnp.dot(vt_scr[h * HEAD_V:(h + 1) * HEAD_V, pl.ds(c0, tk)], pb[:, w],
                         preferred_element_type=F32)
            acc_scr[x, h] = alpha[:, w] * acc_scr[x, h] + pv

    qa = t
    qb = nq - 1 - t
    has_sub = t > 0
    n_far_a = jnp.maximum(t - 1, 0)
    blocks = [(0, qa, DIAG, False), (1, qb, DIAG, False), (1, qb - 1, SUB, False),
              (jnp.where(has_sub, 0, 1), jnp.where(has_sub, qa - 1, nq - 3), jnp.where(has_sub, SUB, FAR), False)]
    for i in range(nq - 3):
        to_a = i < n_far_a
        blocks.append((jnp.where(to_a, 0, 1), jnp.where(to_a, i, i - n_far_a), FAR, True))
    staged = scores(0, *blocks[0])
    for n in range(len(blocks)):
        following = scores(n + 1, *blocks[n + 1]) if n + 1 < len(blocks) else None
        accumulate(*staged)
        staged = following

    lp = lam_ref[...]
    lam = (jnp.exp(jnp.sum(lp[0:1] * lp[1:2], axis=-1, keepdims=True))
           - jnp.exp(jnp.sum(lp[2:3] * lp[3:4], axis=-1, keepdims=True)) + lam_init)
    for x in range(2):
        heads = []
        for h in range(N_HEADS):
            r = acc_scr[x, h] / l_scr[x, :, 2 * h * tq:(2 * h + 2) * tq]
            o_h = r[:, :tq] - lam * r[:, tq:]
            ms = jnp.mean(o_h * o_h, axis=0, keepdims=True)
            heads.append(o_h * lax.rsqrt(ms + EPS) * subg_ref[...] * (1.0 - lam_init))
        o_ref[x] = jnp.concatenate(heads, axis=0).T.astype(o_ref.dtype)


def _attention(qkv, tbl, gq, gk, gsum, lam_params, subg_col, batch, seq, tq, lam_init):
    t = qkv.shape[0]
    nq = seq // tq
    assert nq % 2 == 0 and nq >= 4
    nh = nq // 2
    starts = _bucket_starts(2 * tq)
    assert starts[REL_BUCKETS - 1] <= tq, "blocks two or more tiles below the diagonal must sit in the last bucket"
    const = lambda b, i: (0, 0)
    kern = functools.partial(_attn_kernel, tq=tq, nq=nq, lam_init=lam_init, starts=starts)
    return pl.pallas_call(
        kern,
        grid=(batch, nh),
        in_specs=[
            pl.BlockSpec(memory_space=pltpu.SMEM),
            pl.BlockSpec((tq, GROUP_W), lambda b, i: (b * nq + i, 0)),
            pl.BlockSpec((tq, GROUP_W), lambda b, i: (b * nq + nq - 1 - i, 0)),
            pl.BlockSpec((seq, GROUP_W), lambda b, i: (b, 1)),
            pl.BlockSpec((seq, GROUP_W), lambda b, i: (b, 2)),
            pl.BlockSpec((1, GROUP_W), const),
            pl.BlockSpec((1, GROUP_W), const),
            pl.BlockSpec((GROUP_W, GROUP_W), const),
            pl.BlockSpec((4, HEAD_QK), const),
            pl.BlockSpec((HEAD_V, 1), const),
        ],
        out_specs=pl.BlockSpec((2, tq, GROUP_W), lambda b, i: (0, b * nh + i, 0)),
        out_shape=jax.ShapeDtypeStruct((2, t // 2, GROUP_W), BF16),
        scratch_shapes=[
            pltpu.VMEM((seq, GROUP_W), BF16),
            pltpu.VMEM((GROUP_W, seq), BF16),
            pltpu.VMEM((3, tq, N_MAPS * tq), F32),
            pltpu.VMEM((1, N_MAPS * tq), F32),
            pltpu.VMEM((2, GROUP_W, N_MAPS * tq), BF16),
            pltpu.VMEM((2, 1, N_MAPS * tq), F32),
            pltpu.VMEM((2, 1, N_MAPS * tq), F32),
            pltpu.VMEM((2, N_HEADS, HEAD_V, 2 * tq), F32),
            pltpu.VMEM((1, tq, N_MAPS * tq), F32),
            pltpu.VMEM((1, tq, N_MAPS * tq), F32),
        ],
        compiler_params=pltpu.CompilerParams(
            dimension_semantics=("arbitrary", "arbitrary"), vmem_limit_bytes=VMEM_LIMIT),
        name="diff_attention",
    )(tbl, qkv, qkv, qkv, qkv, gq, gk, gsum, lam_params, subg_col)


def _out_ffn_kernel(x_ref, mix_ref, att0_ref, att1_ref, wo_ref, g2_ref, wg_ref, wu_ref, wd_ref, o_ref, a_scr,
                    *, ff_chunk):
    mix_w = mix_ref.shape[1]
    att = jnp.concatenate([att0_ref[...], att1_ref[...]], axis=0)
    x1 = (x_ref[...]
          + jnp.dot(mix_ref[...], wo_ref[0:mix_w, :], preferred_element_type=F32)
          + jnp.dot(att, wo_ref[mix_w:, :], preferred_element_type=F32))
    ms = jnp.mean(x1 * x1, axis=-1, keepdims=True)
    h2 = (x1 * lax.rsqrt(ms + EPS) * g2_ref[...]).astype(BF16)
    for c0 in range(0, D_FF, ff_chunk):
        gate = jnp.dot(h2, wg_ref[:, c0:c0 + ff_chunk], preferred_element_type=F32)
        up = jnp.dot(h2, wu_ref[:, c0:c0 + ff_chunk], preferred_element_type=F32)
        a_scr[:, c0:c0 + ff_chunk] = (gate * jax.nn.sigmoid(gate) * up).astype(BF16)
    o_ref[...] = x1 + jnp.dot(a_scr[...], wd_ref[...], preferred_element_type=F32)


def _out_ffn(x2d, mix, att, wo, g2, wg, wu, wd, tm, seq, tq):
    t = x2d.shape[0]
    nq = seq // tq
    nh = nq // 2
    assert tm == 2 * tq
    tiles_per_seq = seq // tm
    const = lambda i: (0, 0)
    single = pl.Buffered(1)

    def att_block(which):
        def index(i):
            bb = i // tiles_per_seq
            u = 2 * (i % tiles_per_seq) + which
            lower = u < nh
            return (jnp.where(lower, 0, 1), bb * nh + jnp.where(lower, u, nq - 1 - u), 0)
        return pl.BlockSpec((None, tq, GROUP_W), index)

    kern = functools.partial(_out_ffn_kernel, ff_chunk=256)
    return pl.pallas_call(
        kern,
        grid=(t // tm,),
        in_specs=[
            pl.BlockSpec((tm, D_MODEL), lambda i: (i, 0)),
            pl.BlockSpec((tm, mix.shape[1]), lambda i: (i, 0)),
            att_block(0),
            att_block(1),
            pl.BlockSpec((D_MODEL, D_MODEL), const, pipeline_mode=single),
            pl.BlockSpec((1, D_MODEL), const),
            pl.BlockSpec((D_MODEL, D_FF), const, pipeline_mode=single),
            pl.BlockSpec((D_MODEL, D_FF), const, pipeline_mode=single),
            pl.BlockSpec((D_FF, D_MODEL), const, pipeline_mode=single),
        ],
        out_specs=pl.BlockSpec((tm, D_MODEL), lambda i: (i, 0)),
        out_shape=jax.ShapeDtypeStruct((t, D_MODEL), F32),
        scratch_shapes=[pltpu.VMEM((tm, D_FF), BF16)],
        compiler_params=pltpu.CompilerParams(
            dimension_semantics=("arbitrary",), vmem_limit_bytes=VMEM_LIMIT),
        name="out_ffn",
    )(x2d, mix, att, att, wo, g2, wg, wu, wd)


def _block_diag_pool(pool_w):
    out = jnp.zeros((GROUP_W, GROUP_W), pool_w.dtype)
    for g in range(len(POOL_WINDOWS)):
        out = out.at[g * POOL_GROUP:(g + 1) * POOL_GROUP, g * POOL_GROUP:(g + 1) * POOL_GROUP].set(pool_w[g])
    return out


def kernel(x, norm1_g, w_in, pool_w, pool_scale, sconv_w, conf_dw_w, conf_dw_b, conf_ln_g, conf_ln_b,
           q_norm_g, k_norm_g, lam_q1, lam_k1, lam_q2, lam_k2, subln_g, w_out, norm2_g, w_gate, w_up,
           w_down, rel_bias):
    batch, seq, d = x.shape
    assert d == D_MODEL and w_in.shape[-1] == D_IN
    depth = w_in.shape[0]
    t = batch * seq
    tq = 256
    tm = 2 * tq
    ts = 256
    assert seq % tm == 0 and seq % ts == 0 and ts % HALO == 0

    gid = np.arange(GROUP_W) // HEAD_QK
    gsum = jnp.asarray((gid[:, None] == gid[None, :]).astype(np.float32), BF16)
    tbl = (rel_bias.astype(F32).T * LOG2E).reshape(-1)
    row = lambda v: v.astype(F32).reshape(1, -1)

    x2d = x.reshape(t, D_MODEL)
    for l in range(depth):
        lam_init = 0.8 - 0.6 * math.exp(-0.3 * l)
        qkv, mix = _in_mix(x2d, row(norm1_g[l]), w_in[l].astype(BF16), _block_diag_pool(pool_w[l]).astype(BF16),
                           row(pool_scale[l]), sconv_w[l].astype(F32), conf_dw_w[l].astype(F32), row(conf_dw_b[l]),
                           row(conf_ln_g[l]), row(conf_ln_b[l]), batch, seq, ts)
        gq = row(jnp.tile(q_norm_g[l].astype(F32), N_MAPS) * (HEAD_QK ** -0.5 * LOG2E))
        gk = row(jnp.tile(k_norm_g[l].astype(F32), N_MAPS))
        lam_params = jnp.stack([lam_q1[l], lam_k1[l], lam_q2[l], lam_k2[l]]).astype(F32)
        att = _attention(qkv, tbl, gq, gk, gsum, lam_params, subln_g[l].astype(F32).reshape(-1, 1), batch, seq, tq,
                         lam_init)
        x2d = _out_ffn(x2d, mix, att, w_out[l].astype(BF16), row(norm2_g[l]), w_gate[l].astype(BF16),
                       w_up[l].astype(BF16), w_down[l].astype(BF16), tm, seq, tq)
    return x2d.reshape(batch, seq, D_MODEL)
```

```python
import functools
import math

import numpy as np
import jax
import jax.numpy as jnp
from jax import lax
from jax.experimental import pallas as pl
from jax.experimental.pallas import tpu as pltpu

F32 = jnp.float32
BF16 = jnp.bfloat16

D_MODEL = 1024
GROUP_W = 256
N_SLICES = 9
D_IN = N_SLICES * GROUP_W
N_MIX_SLICES = 6
POOL_WINDOWS = (2, 4, 8, 16)
POOL_GROUP = GROUP_W // len(POOL_WINDOWS)
SCONV_K = 3
CONF_K = 31
N_HEADS = 4
HEAD_V = GROUP_W // N_HEADS
HEAD_QK = HEAD_V // 2
V_ROWS = HEAD_V + 16
N_MAPS = 2 * N_HEADS
D_FF = 2816
REL_BUCKETS = 32
REL_MAX_EXACT = 16
REL_MAX_DIST = 128
EPS = 1e-6
NEG_INF = -1e30
LOG2E = math.log2(math.e)

HALO = 32
VMEM_LIMIT = 56 * 1024 * 1024


def _bucket_starts(max_dist):
    n = np.arange(0, max_dist + 1)
    nf = np.maximum(n, 1).astype(np.float32)
    large = REL_MAX_EXACT + (np.log(nf / np.float32(REL_MAX_EXACT)) / np.float32(math.log(REL_MAX_DIST / REL_MAX_EXACT))
                             * np.float32(REL_BUCKETS - REL_MAX_EXACT)).astype(np.int32)
    large = np.minimum(large, REL_BUCKETS - 1)
    bucket = np.where(n < REL_MAX_EXACT, n, large)
    starts = [int(np.argmax(bucket >= b)) for b in range(REL_BUCKETS)]
    assert bucket[-1] == REL_BUCKETS - 1
    return starts


def _in_mix_kernel(x_ref, g1_ref, w_ref, poolw_ref, pscale_ref, sconvw_ref, cw_ref, cb_ref, lng_ref, lnb_ref,
                   qkv_ref, mix_ref, zbuf, a0, b1, b2, b3, p_ext, u_ext, u_sh, h_scr, *, ts, nst, row_chunk):
    n = HALO + ts
    W = GROUP_W
    g = pl.program_id(0)
    sm = lax.rem(g + nst - 1, nst)

    @pl.when(g == 0)
    def _():
        zbuf[...] = jnp.zeros(zbuf.shape, F32)

    @pl.when(sm == 0)
    def _():
        a0[0:HALO, :] = jnp.zeros((HALO, W), F32)
        p_ext[0:8, :] = jnp.zeros((8, W), F32)
        u_ext[0:HALO, :] = jnp.zeros((HALO, W), F32)

    a0[HALO:n, :] = zbuf[:, 0:W]
    p_ext[8:8 + ts, :] = zbuf[:, 2 * W:3 * W] * zbuf[:, 3 * W:4 * W]
    conv3 = sconvw_ref[SCONV_K - 1:SCONV_K, :] * p_ext[8:8 + ts, :]
    for j in range(1, SCONV_K):
        conv3 = conv3 + sconvw_ref[SCONV_K - 1 - j:SCONV_K - j, :] * p_ext[8 - j:8 - j + ts, :]
    mix_ref[:, W:2 * W] = (zbuf[:, W:2 * W] * conv3).astype(mix_ref.dtype)
    p_ext[0:8, :] = p_ext[ts:ts + 8, :]
    u_ext[HALO:n, :] = zbuf[:, 4 * W:5 * W] * jax.nn.sigmoid(zbuf[:, 5 * W:6 * W])
    for r in range(1, 8):
        u_sh[r, 8:n, :] = u_ext[8 - r:n - r, :]

    x = x_ref[...]
    ms = jnp.mean(x * x, axis=-1, keepdims=True)
    h = h_scr.at[jnp.minimum(g, 0)]
    h[...] = (x_ref[...] * lax.rsqrt(ms + EPS) * g1_ref[...]).astype(BF16)

    def project(k):
        zk = jnp.dot(h[...], w_ref[:, k * W:(k + 1) * W], preferred_element_type=F32)
        if k < N_MIX_SLICES:
            zbuf[:, k * W:(k + 1) * W] = zk
        else:
            qkv_ref[:, (k - N_MIX_SLICES) * W:(k - N_MIX_SLICES + 1) * W] = zk.astype(qkv_ref.dtype)

    def rows8(v):
        return v.reshape(v.shape[0] // 8, 8, W)

    chunks = list(range(0, ts, row_chunk))
    assert len(chunks) >= N_SLICES - 1
    for ci, r0 in enumerate(chunks):
        if ci < N_SLICES - 1:
            project(ci)
        acc = cb_ref[...] + cw_ref[CONF_K - 1] * rows8(u_ext[HALO + r0:HALO + r0 + row_chunk, :])
        for j in range(1, CONF_K):
            q8, r = 8 * (j // 8), j % 8
            lo = HALO + r0 - q8
            src = u_ext[lo:lo + row_chunk, :] if r == 0 else u_sh[r, lo:lo + row_chunk, :]
            acc = acc + cw_ref[CONF_K - 1 - j] * rows8(src)
        mu = jnp.mean(acc, axis=-1, keepdims=True)
        xc = acc - mu
        var = jnp.mean(xc * xc, axis=-1, keepdims=True)
        yn = xc * lax.rsqrt(var + EPS) * lng_ref[...] + lnb_ref[...]
        mix_ref[r0:r0 + row_chunk, 2 * W:3 * W] = (
            (yn * jax.nn.sigmoid(yn)).reshape(row_chunk, W).astype(mix_ref.dtype))
    u_ext[0:HALO, :] = u_ext[ts:n, :]
    project(N_SLICES - 1)

    b1[8:n, :] = a0[8:n, :] + a0[7:n - 1, :]
    b2[16:n, :] = b1[16:n, :] + b1[14:n - 2, :]
    b3[24:n, :] = b2[24:n, :] + b2[20:n - 4, :]
    s16 = b3[HALO:n, :] + b3[HALO - 8:n - 8, :]
    lane = lax.broadcasted_iota(jnp.int32, (ts, W), 1)
    pos1 = sm * ts + lax.broadcasted_iota(jnp.int32, (ts, W), 0) + 1
    sel = jnp.where(lane < POOL_GROUP, b1[HALO:n, :],
                    jnp.where(lane < 2 * POOL_GROUP, b2[HALO:n, :],
                              jnp.where(lane < 3 * POOL_GROUP, b3[HALO:n, :], s16)))
    win = jnp.where(lane < POOL_GROUP, POOL_WINDOWS[0],
                    jnp.where(lane < 2 * POOL_GROUP, POOL_WINDOWS[1],
                              jnp.where(lane < 3 * POOL_GROUP, POOL_WINDOWS[2], POOL_WINDOWS[3])))
    cnt = jnp.minimum(pos1, win).astype(F32)
    pooled = sel / cnt - a0[HALO:n, :]
    y_pool = jnp.dot(pooled.astype(BF16), poolw_ref[...], preferred_element_type=F32) * pscale_ref[...]
    mix_ref[:, 0:W] = y_pool.astype(mix_ref.dtype)
    a0[0:HALO, :] = a0[ts:n, :]


def _in_mix(x2d, g1, w, poolw_bd, pscale, sconvw, cw, cb, lng, lnb, seq, ts):
    t = x2d.shape[0]
    nst = seq // ts
    n_tiles = t // ts
    const = lambda g: (0, 0)
    kern = functools.partial(_in_mix_kernel, ts=ts, nst=nst, row_chunk=32)
    out_w = 3 * GROUP_W
    return pl.pallas_call(
        kern,
        grid=(n_tiles + 1,),
        in_specs=[
            pl.BlockSpec((ts, D_MODEL), lambda g: (jnp.minimum(g, n_tiles - 1), 0)),
            pl.BlockSpec((1, D_MODEL), const),
            pl.BlockSpec((D_MODEL, D_IN), const, pipeline_mode=pl.Buffered(1)),
            pl.BlockSpec((GROUP_W, GROUP_W), const),
            pl.BlockSpec((1, GROUP_W), const),
            pl.BlockSpec((SCONV_K, GROUP_W), const),
            pl.BlockSpec((CONF_K, 8, GROUP_W), lambda g: (0, 0, 0)),
            pl.BlockSpec((8, GROUP_W), const),
            pl.BlockSpec((8, GROUP_W), const),
            pl.BlockSpec((8, GROUP_W), const),
        ],
        out_specs=[pl.BlockSpec((ts, out_w), lambda g: (jnp.minimum(g, n_tiles - 1), 0)),
                   pl.BlockSpec((ts, out_w), lambda g: (jnp.maximum(g - 1, 0), 0))],
        out_shape=[jax.ShapeDtypeStruct((t, out_w), BF16), jax.ShapeDtypeStruct((t, out_w), BF16)],
        scratch_shapes=[
            pltpu.VMEM((ts, N_MIX_SLICES * GROUP_W), F32),
            pltpu.VMEM((HALO + ts, GROUP_W), F32),
            pltpu.VMEM((HALO + ts, GROUP_W), F32),
            pltpu.VMEM((HALO + ts, GROUP_W), F32),
            pltpu.VMEM((HALO + ts, GROUP_W), F32),
            pltpu.VMEM((8 + ts, GROUP_W), F32),
            pltpu.VMEM((HALO + ts, GROUP_W), F32),
            pltpu.VMEM((8, HALO + ts, GROUP_W), F32),
            pltpu.VMEM((1, ts, D_MODEL), BF16),
        ],
        compiler_params=pltpu.CompilerParams(
            dimension_semantics=("arbitrary",), vmem_limit_bytes=VMEM_LIMIT),
        name="in_mix",
    )(x2d, g1, w, poolw_bd, pscale, sconvw, cw, cb, lng, lnb)


def _group_rms(x, gsum_ref):
    x2 = x * x
    hi = x2.astype(BF16)
    lo = (x2 - hi.astype(F32)).astype(BF16)
    ssq = (jnp.dot(hi, gsum_ref[...], preferred_element_type=F32)
           + jnp.dot(lo, gsum_ref[...], preferred_element_type=F32))
    return lax.rsqrt(ssq * (1.0 / HEAD_QK) + EPS)


def _attn_kernel(tbl_ref, qa_ref, qb_ref, k_ref, v_ref, gq_ref, gk_ref, gsum_ref, lam_ref, subg_ref,
                 o_ref, kn_scr, vt_scr, bias_scr, cfar_scr, qm_scr, m_scr, acc_scr, s0_scr, s1_scr,
                 *, tq, nq, lam_init, starts):
    b = pl.program_id(0)
    t = pl.program_id(1)
    tk = tq
    nw = N_MAPS * tq
    seq = k_ref.shape[0]
    DIAG, SUB, FAR = 0, 1, 2

    @pl.when((b == 0) & (t == 0))
    def _():
        kk = lax.broadcasted_iota(jnp.int32, (tk, tq), 0)
        qq = lax.broadcasted_iota(jnp.int32, (tk, tq), 1)
        for h in range(N_HEADS):
            far = tbl_ref[h * REL_BUCKETS + REL_BUCKETS - 1]
            cfar_scr[:, 2 * h * tq:(2 * h + 2) * tq] = jnp.full((1, 2 * tq), far, F32)
            bias_scr[FAR, :, 2 * h * tq:(2 * h + 2) * tq] = jnp.full((tk, 2 * tq), far, F32)
            for off in (DIAG, SUB):
                d = qq - kk + off * tk
                val = jnp.full((tk, tq), tbl_ref[h * REL_BUCKETS], F32)
                for bk in range(1, REL_BUCKETS):
                    val = jnp.where(d >= starts[bk], tbl_ref[h * REL_BUCKETS + bk], val)
                if off == DIAG:
                    val = jnp.where(d >= 0, val, NEG_INF)
                for mp in range(2):
                    bias_scr[off, :, (2 * h + mp) * tq:(2 * h + mp + 1) * tq] = val

    @pl.when(t == 0)
    def _():
        for c0 in range(0, seq, tk):
            kf = k_ref[c0:c0 + tk, :].astype(F32)
            kn_scr[c0:c0 + tk, :] = (kf * _group_rms(kf, gsum_ref) * gk_ref[...]).astype(BF16)
            vt = v_ref[c0:c0 + tk, :].astype(F32).T.astype(BF16)
            for h in range(N_HEADS):
                vt_scr[h, 0:HEAD_V, c0:c0 + tk] = vt[h * HEAD_V:(h + 1) * HEAD_V, :]
                vt_scr[h, HEAD_V:V_ROWS, c0:c0 + tk] = jnp.ones((V_ROWS - HEAD_V, tk), BF16)

    grp = lax.broadcasted_iota(jnp.int32, (GROUP_W, tq), 0) // HEAD_QK
    for x, q_ref in enumerate((qa_ref, qb_ref)):
        qf = q_ref[...].astype(F32)
        qt = (qf * _group_rms(qf, gsum_ref) * gq_ref[...]).T
        for g in range(N_MAPS):
            qm_scr[x, :, g * tq:(g + 1) * tq] = jnp.where(grp == g, qt, 0.0).astype(BF16)
    m_scr[...] = jnp.full(m_scr.shape, NEG_INF, F32)
    acc_scr[...] = jnp.zeros(acc_scr.shape, F32)

    def scores(n, x, j, bias_tile, constant_bias):
        sbuf = (s0_scr, s1_scr)[n % 2].at[jnp.minimum(t, 0)]
        c0 = pl.multiple_of(j * tk, tk)
        s = jnp.dot(kn_scr[pl.ds(c0, tk), :], qm_scr[x], preferred_element_type=F32)
        m_old = m_scr[x]
        if constant_bias:
            sbuf[...] = s
            c = cfar_scr[...]
            m_new = jnp.maximum(m_old, jnp.max(s, axis=0, keepdims=True) + c)
            shift = m_new - c
        else:
            s = s + bias_scr[bias_tile]
            sbuf[...] = s
            m_new = jnp.maximum(m_old, jnp.max(s, axis=0, keepdims=True))
            shift = m_new
        m_scr[x] = m_new
        return x, c0, sbuf, shift, jnp.exp2(m_old - m_new)

    def accumulate(x, c0, sbuf, shift, alpha):
        pb = jnp.exp2(sbuf[...] - shift).astype(BF16)
        for h in range(N_HEADS):
            w = slice(2 * h * tq, (2 * h + 2) * tq)
            pv = jnp.dot(vt_scr[h, :, pl.ds(c0, tk)], pb[:, w], preferred_element_type=F32)
            acc_scr[x, h] = alpha[:, w] * acc_scr[x, h] + pv

    qa = t
    qb = nq - 1 - t
    has_sub = t > 0
    n_far_a = jnp.maximum(t - 1, 0)
    blocks = [(0, qa, DIAG, False), (1, qb, DIAG, False), (1, qb - 1, SUB, False),
              (jnp.where(has_sub, 0, 1), jnp.where(has_sub, qa - 1, nq - 3), jnp.where(has_sub, SUB, FAR), False)]
    for i in range(nq - 3):
        to_a = i < n_far_a
        blocks.append((jnp.where(to_a, 0, 1), jnp.where(to_a, i, i - n_far_a), FAR, True))
    staged = scores(0, *blocks[0])
    for n in range(len(blocks)):
        following = scores(n + 1, *blocks[n + 1]) if n + 1 < len(blocks) else None
        accumulate(*staged)
        staged = following

    lp = lam_ref[...]
    lam = (jnp.exp(jnp.sum(lp[0:1] * lp[1:2], axis=-1, keepdims=True))
           - jnp.exp(jnp.sum(lp[2:3] * lp[3:4], axis=-1, keepdims=True)) + lam_init)
    for x in range(2):
        heads = []
        for h in range(N_HEADS):
            r = acc_scr[x, h, 0:HEAD_V, :] / acc_scr[x, h, HEAD_V:HEAD_V + 1, :]
            o_h = r[:, :tq] - lam * r[:, tq:]
            ms = jnp.mean(o_h * o_h, axis=0, keepdims=True)
            heads.append(o_h * lax.rsqrt(ms + EPS) * subg_ref[...] * (1.0 - lam_init))
        o_ref[x] = jnp.concatenate(heads, axis=0).T.astype(o_ref.dtype)


def _attention(qkv, tbl, gq, gk, gsum, lam_params, subg_col, batch, seq, tq, lam_init):
    t = qkv.shape[0]
    nq = seq // tq
    assert nq % 2 == 0 and nq >= 4
    nh = nq // 2
    starts = _bucket_starts(2 * tq)
    assert starts[REL_BUCKETS - 1] <= tq, "blocks two or more tiles below the diagonal must sit in the last bucket"
    const = lambda b, i: (0, 0)
    kern = functools.partial(_attn_kernel, tq=tq, nq=nq, lam_init=lam_init, starts=starts)
    return pl.pallas_call(
        kern,
        grid=(batch, nh),
        in_specs=[
            pl.BlockSpec(memory_space=pltpu.SMEM),
            pl.BlockSpec((tq, GROUP_W), lambda b, i: (b * nq + i, 0)),
            pl.BlockSpec((tq, GROUP_W), lambda b, i: (b * nq + nq - 1 - i, 0)),
            pl.BlockSpec((seq, GROUP_W), lambda b, i: (b, 1)),
            pl.BlockSpec((seq, GROUP_W), lambda b, i: (b, 2)),
            pl.BlockSpec((1, GROUP_W), const),
            pl.BlockSpec((1, GROUP_W), const),
            pl.BlockSpec((GROUP_W, GROUP_W), const),
            pl.BlockSpec((4, HEAD_QK), const),
            pl.BlockSpec((HEAD_V, 1), const),
        ],
        out_specs=pl.BlockSpec((2, tq, GROUP_W), lambda b, i: (0, b * nh + i, 0)),
        out_shape=jax.ShapeDtypeStruct((2, t // 2, GROUP_W), BF16),
        scratch_shapes=[
            pltpu.VMEM((seq, GROUP_W), BF16),
            pltpu.VMEM((N_HEADS, V_ROWS, seq), BF16),
            pltpu.VMEM((3, tq, N_MAPS * tq), F32),
            pltpu.VMEM((1, N_MAPS * tq), F32),
            pltpu.VMEM((2, GROUP_W, N_MAPS * tq), BF16),
            pltpu.VMEM((2, 1, N_MAPS * tq), F32),
            pltpu.VMEM((2, N_HEADS, V_ROWS, 2 * tq), F32),
            pltpu.VMEM((1, tq, N_MAPS * tq), F32),
            pltpu.VMEM((1, tq, N_MAPS * tq), F32),
        ],
        compiler_params=pltpu.CompilerParams(
            dimension_semantics=("arbitrary", "arbitrary"), vmem_limit_bytes=VMEM_LIMIT),
        name="diff_attention",
    )(tbl, qkv, qkv, qkv, qkv, gq, gk, gsum, lam_params, subg_col)


def _out_ffn_kernel(x_ref, mix_ref, att0_ref, att1_ref, wo_ref, g2_ref, wg_ref, wu_ref, wd_ref, o_ref, a_scr,
                    *, ff_chunk):
    mix_w = mix_ref.shape[1]
    att = jnp.concatenate([att0_ref[...], att1_ref[...]], axis=0)
    x1 = (x_ref[...]
          + jnp.dot(mix_ref[...], wo_ref[0:mix_w, :], preferred_element_type=F32)
          + jnp.dot(att, wo_ref[mix_w:, :], preferred_element_type=F32))
    ms = jnp.mean(x1 * x1, axis=-1, keepdims=True)
    h2 = (x1 * lax.rsqrt(ms + EPS) * g2_ref[...]).astype(BF16)
    for c0 in range(0, D_FF, ff_chunk):
        gate = jnp.dot(h2, wg_ref[:, c0:c0 + ff_chunk], preferred_element_type=F32)
        up = jnp.dot(h2, wu_ref[:, c0:c0 + ff_chunk], preferred_element_type=F32)
        a_scr[:, c0:c0 + ff_chunk] = (gate * jax.nn.sigmoid(gate) * up).astype(BF16)
    o_ref[...] = x1 + jnp.dot(a_scr[...], wd_ref[...], preferred_element_type=F32)


def _out_ffn(x2d, mix, att, wo, g2, wg, wu, wd, tm, seq, tq):
    t = x2d.shape[0]
    nq = seq // tq
    nh = nq // 2
    assert tm == 2 * tq
    tiles_per_seq = seq // tm
    const = lambda i: (0, 0)
    single = pl.Buffered(1)

    def att_block(which):
        def index(i):
            bb = i // tiles_per_seq
            u = 2 * (i % tiles_per_seq) + which
            lower = u < nh
            return (jnp.where(lower, 0, 1), bb * nh + jnp.where(lower, u, nq - 1 - u), 0)
        return pl.BlockSpec((None, tq, GROUP_W), index)

    kern = functools.partial(_out_ffn_kernel, ff_chunk=256)
    return pl.pallas_call(
        kern,
        grid=(t // tm,),
        in_specs=[
            pl.BlockSpec((tm, D_MODEL), lambda i: (i, 0)),
            pl.BlockSpec((tm, mix.shape[1]), lambda i: (i, 0)),
            att_block(0),
            att_block(1),
            pl.BlockSpec((D_MODEL, D_MODEL), const, pipeline_mode=single),
            pl.BlockSpec((1, D_MODEL), const),
            pl.BlockSpec((D_MODEL, D_FF), const, pipeline_mode=single),
            pl.BlockSpec((D_MODEL, D_FF), const, pipeline_mode=single),
            pl.BlockSpec((D_FF, D_MODEL), const, pipeline_mode=single),
        ],
        out_specs=pl.BlockSpec((tm, D_MODEL), lambda i: (i, 0)),
        out_shape=jax.ShapeDtypeStruct((t, D_MODEL), F32),
        scratch_shapes=[pltpu.VMEM((tm, D_FF), BF16)],
        compiler_params=pltpu.CompilerParams(
            dimension_semantics=("arbitrary",), vmem_limit_bytes=VMEM_LIMIT),
        name="out_ffn",
    )(x2d, mix, att, att, wo, g2, wg, wu, wd)


def _block_diag_pool(pool_w):
    out = jnp.zeros((GROUP_W, GROUP_W), pool_w.dtype)
    for g in range(len(POOL_WINDOWS)):
        out = out.at[g * POOL_GROUP:(g + 1) * POOL_GROUP, g * POOL_GROUP:(g + 1) * POOL_GROUP].set(pool_w[g])
    return out


def kernel(x, norm1_g, w_in, pool_w, pool_scale, sconv_w, conf_dw_w, conf_dw_b, conf_ln_g, conf_ln_b,
           q_norm_g, k_norm_g, lam_q1, lam_k1, lam_q2, lam_k2, subln_g, w_out, norm2_g, w_gate, w_up,
           w_down, rel_bias):
    batch, seq, d = x.shape
    assert d == D_MODEL and w_in.shape[-1] == D_IN
    depth = w_in.shape[0]
    t = batch * seq
    tq = 256
    tm = 2 * tq
    ts = 256
    assert seq % tm == 0 and seq % ts == 0 and ts % HALO == 0

    gid = np.arange(GROUP_W) // HEAD_QK
    gsum = jnp.asarray((gid[:, None] == gid[None, :]).astype(np.float32), BF16)
    tbl = (rel_bias.astype(F32).T * LOG2E).reshape(-1)
    row = lambda v: v.astype(F32).reshape(1, -1)
    rep8 = lambda v: jnp.broadcast_to(v.astype(F32)[..., None, :], v.shape[:-1] + (8, v.shape[-1]))

    x2d = x.reshape(t, D_MODEL)
    for l in range(depth):
        lam_init = 0.8 - 0.6 * math.exp(-0.3 * l)
        qkv, mix = _in_mix(x2d, row(norm1_g[l]), w_in[l].astype(BF16), _block_diag_pool(pool_w[l]).astype(BF16),
                           row(pool_scale[l]), sconv_w[l].astype(F32), rep8(conf_dw_w[l]), rep8(conf_dw_b[l]),
                           rep8(conf_ln_g[l]), rep8(conf_ln_b[l]), seq, ts)
        gq = row(jnp.tile(q_norm_g[l].astype(F32), N_MAPS) * (HEAD_QK ** -0.5 * LOG2E))
        gk = row(jnp.tile(k_norm_g[l].astype(F32), N_MAPS))
        lam_params = jnp.stack([lam_q1[l], lam_k1[l], lam_q2[l], lam_k2[l]]).astype(F32)
        att = _attention(qkv, tbl, gq, gk, gsum, lam_params, subln_g[l].astype(F32).reshape(-1, 1), batch, seq, tq,
                         lam_init)
        x2d = _out_ffn(x2d, mix, att, w_out[l].astype(BF16), row(norm2_g[l]), w_gate[l].astype(BF16),
                       w_up[l].astype(BF16), w_down[l].astype(BF16), tm, seq, tq)
    return x2d.reshape(batch, seq, D_MODEL)
```

```python
import functools
import math

import numpy as np
import jax
import jax.numpy as jnp
from jax import lax
from jax.experimental import pallas as pl
from jax.experimental.pallas import tpu as pltpu

F32 = jnp.float32
BF16 = jnp.bfloat16

D_MODEL = 1024
GROUP_W = 256
N_SLICES = 9
D_IN = N_SLICES * GROUP_W
N_MIX_SLICES = 6
POOL_WINDOWS = (2, 4, 8, 16)
POOL_GROUP = GROUP_W // len(POOL_WINDOWS)
SCONV_K = 3
CONF_K = 31
N_HEADS = 4
HEAD_V = GROUP_W // N_HEADS
HEAD_QK = HEAD_V // 2
V_ROWS = HEAD_V + 16
N_MAPS = 2 * N_HEADS
D_FF = 2816
REL_BUCKETS = 32
REL_MAX_EXACT = 16
REL_MAX_DIST = 128
EPS = 1e-6
NEG_INF = -1e30
LOG2E = math.log2(math.e)

HALO = 32
VMEM_LIMIT = 56 * 1024 * 1024


def _bucket_starts(max_dist):
    n = np.arange(0, max_dist + 1)
    nf = np.maximum(n, 1).astype(np.float32)
    large = REL_MAX_EXACT + (np.log(nf / np.float32(REL_MAX_EXACT)) / np.float32(math.log(REL_MAX_DIST / REL_MAX_EXACT))
                             * np.float32(REL_BUCKETS - REL_MAX_EXACT)).astype(np.int32)
    large = np.minimum(large, REL_BUCKETS - 1)
    bucket = np.where(n < REL_MAX_EXACT, n, large)
    starts = [int(np.argmax(bucket >= b)) for b in range(REL_BUCKETS)]
    assert bucket[-1] == REL_BUCKETS - 1
    return starts


def _layer_block(l, shape, **kwargs):
    index = (l,) + (0,) * len(shape)
    return pl.BlockSpec((None,) + tuple(shape), lambda *g: index, **kwargs)


def _in_mix_kernel(x_ref, g1_ref, w_ref, poolw_ref, pscale_ref, sconvw_ref, cw_ref, cb_ref, lng_ref, lnb_ref,
                   qkv_ref, mix_ref, zbuf, a0, b1, b2, b3, p_ext, u_ext, u_sh, h_scr, *, ts, nst, row_chunk):
    n = HALO + ts
    W = GROUP_W
    g = pl.program_id(0)
    sm = lax.rem(g + nst - 1, nst)

    @pl.when(g == 0)
    def _():
        zbuf[...] = jnp.zeros(zbuf.shape, F32)

    @pl.when(sm == 0)
    def _():
        a0[0:HALO, :] = jnp.zeros((HALO, W), F32)
        p_ext[0:8, :] = jnp.zeros((8, W), F32)
        u_ext[0:HALO, :] = jnp.zeros((HALO, W), F32)

    a0[HALO:n, :] = zbuf[:, 0:W]
    p_ext[8:8 + ts, :] = zbuf[:, 2 * W:3 * W] * zbuf[:, 3 * W:4 * W]
    conv3 = sconvw_ref[SCONV_K - 1:SCONV_K, :] * p_ext[8:8 + ts, :]
    for j in range(1, SCONV_K):
        conv3 = conv3 + sconvw_ref[SCONV_K - 1 - j:SCONV_K - j, :] * p_ext[8 - j:8 - j + ts, :]
    mix_ref[:, W:2 * W] = (zbuf[:, W:2 * W] * conv3).astype(mix_ref.dtype)
    p_ext[0:8, :] = p_ext[ts:ts + 8, :]
    u_ext[HALO:n, :] = zbuf[:, 4 * W:5 * W] * jax.nn.sigmoid(zbuf[:, 5 * W:6 * W])
    for r in range(1, 8):
        u_sh[r, 8:n, :] = u_ext[8 - r:n - r, :]

    x = x_ref[...]
    ms = jnp.mean(x * x, axis=-1, keepdims=True)
    h = h_scr.at[jnp.minimum(g, 0)]
    h[...] = (x_ref[...] * lax.rsqrt(ms + EPS) * g1_ref[...]).astype(BF16)

    def project(k):
        zk = jnp.dot(h[...], w_ref[:, k * W:(k + 1) * W], preferred_element_type=F32)
        if k < N_MIX_SLICES:
            zbuf[:, k * W:(k + 1) * W] = zk
        else:
            qkv_ref[:, (k - N_MIX_SLICES) * W:(k - N_MIX_SLICES + 1) * W] = zk.astype(qkv_ref.dtype)

    def rows8(v):
        return v.reshape(v.shape[0] // 8, 8, W)

    chunks = list(range(0, ts, row_chunk))
    assert len(chunks) >= N_SLICES - 1
    for ci, r0 in enumerate(chunks):
        if ci < N_SLICES - 1:
            project(ci)
        acc = cb_ref[...] + cw_ref[CONF_K - 1] * rows8(u_ext[HALO + r0:HALO + r0 + row_chunk, :])
        for j in range(1, CONF_K):
            q8, r = 8 * (j // 8), j % 8
            lo = HALO + r0 - q8
            src = u_ext[lo:lo + row_chunk, :] if r == 0 else u_sh[r, lo:lo + row_chunk, :]
            acc = acc + cw_ref[CONF_K - 1 - j] * rows8(src)
        mu = jnp.mean(acc, axis=-1, keepdims=True)
        xc = acc - mu
        var = jnp.mean(xc * xc, axis=-1, keepdims=True)
        yn = xc * lax.rsqrt(var + EPS) * lng_ref[...] + lnb_ref[...]
        mix_ref[r0:r0 + row_chunk, 2 * W:3 * W] = (
            (yn * jax.nn.sigmoid(yn)).reshape(row_chunk, W).astype(mix_ref.dtype))
    u_ext[0:HALO, :] = u_ext[ts:n, :]
    project(N_SLICES - 1)

    b1[8:n, :] = a0[8:n, :] + a0[7:n - 1, :]
    b2[16:n, :] = b1[16:n, :] + b1[14:n - 2, :]
    b3[24:n, :] = b2[24:n, :] + b2[20:n - 4, :]
    s16 = b3[HALO:n, :] + b3[HALO - 8:n - 8, :]
    lane = lax.broadcasted_iota(jnp.int32, (ts, W), 1)
    pos1 = sm * ts + lax.broadcasted_iota(jnp.int32, (ts, W), 0) + 1
    sel = jnp.where(lane < POOL_GROUP, b1[HALO:n, :],
                    jnp.where(lane < 2 * POOL_GROUP, b2[HALO:n, :],
                              jnp.where(lane < 3 * POOL_GROUP, b3[HALO:n, :], s16)))
    win = jnp.where(lane < POOL_GROUP, POOL_WINDOWS[0],
                    jnp.where(lane < 2 * POOL_GROUP, POOL_WINDOWS[1],
                              jnp.where(lane < 3 * POOL_GROUP, POOL_WINDOWS[2], POOL_WINDOWS[3])))
    cnt = jnp.minimum(pos1, win).astype(F32)
    pooled = sel / cnt - a0[HALO:n, :]
    y_pool = jnp.dot(pooled.astype(BF16), poolw_ref[...], preferred_element_type=F32) * pscale_ref[...]
    mix_ref[:, 0:W] = y_pool.astype(mix_ref.dtype)
    a0[0:HALO, :] = a0[ts:n, :]


def _in_mix(l, x2d, g1, w, poolw_bd, pscale, sconvw, cw, cb, lng, lnb, seq, ts):
    t = x2d.shape[0]
    nst = seq // ts
    n_tiles = t // ts
    whole = functools.partial(_layer_block, l)
    kern = functools.partial(_in_mix_kernel, ts=ts, nst=nst, row_chunk=32)
    out_w = 3 * GROUP_W
    return pl.pallas_call(
        kern,
        grid=(n_tiles + 1,),
        in_specs=[
            pl.BlockSpec((ts, D_MODEL), lambda g: (jnp.minimum(g, n_tiles - 1), 0)),
            whole((1, D_MODEL)),
            whole((D_MODEL, D_IN), pipeline_mode=pl.Buffered(1)),
            whole((GROUP_W, GROUP_W)),
            whole((1, GROUP_W)),
            whole((SCONV_K, GROUP_W)),
            whole((CONF_K, 8, GROUP_W)),
            whole((8, GROUP_W)),
            whole((8, GROUP_W)),
            whole((8, GROUP_W)),
        ],
        out_specs=[pl.BlockSpec((ts, out_w), lambda g: (jnp.minimum(g, n_tiles - 1), 0)),
                   pl.BlockSpec((ts, out_w), lambda g: (jnp.maximum(g - 1, 0), 0))],
        out_shape=[jax.ShapeDtypeStruct((t, out_w), BF16), jax.ShapeDtypeStruct((t, out_w), BF16)],
        scratch_shapes=[
            pltpu.VMEM((ts, N_MIX_SLICES * GROUP_W), F32),
            pltpu.VMEM((HALO + ts, GROUP_W), F32),
            pltpu.VMEM((HALO + ts, GROUP_W), F32),
            pltpu.VMEM((HALO + ts, GROUP_W), F32),
            pltpu.VMEM((HALO + ts, GROUP_W), F32),
            pltpu.VMEM((8 + ts, GROUP_W), F32),
            pltpu.VMEM((HALO + ts, GROUP_W), F32),
            pltpu.VMEM((8, HALO + ts, GROUP_W), F32),
            pltpu.VMEM((1, ts, D_MODEL), BF16),
        ],
        compiler_params=pltpu.CompilerParams(
            dimension_semantics=("arbitrary",), vmem_limit_bytes=VMEM_LIMIT),
        name="in_mix",
    )(x2d, g1, w, poolw_bd, pscale, sconvw, cw, cb, lng, lnb)


def _group_rms(x, gsum_ref):
    x2 = x * x
    hi = x2.astype(BF16)
    lo = (x2 - hi.astype(F32)).astype(BF16)
    ssq = (jnp.dot(hi, gsum_ref[...], preferred_element_type=F32)
           + jnp.dot(lo, gsum_ref[...], preferred_element_type=F32))
    return lax.rsqrt(ssq * (1.0 / HEAD_QK) + EPS)


def _attn_kernel(tbl_ref, qa_ref, qb_ref, k_ref, v_ref, gq_ref, gk_ref, gsum_ref, lam_ref, subg_ref,
                 o_ref, kn_scr, vt_scr, bias_scr, cfar_scr, qm_scr, m_scr, acc_scr, s0_scr, s1_scr,
                 *, tq, nq, lam_init, starts):
    b = pl.program_id(0)
    t = pl.program_id(1)
    tk = tq
    nw = N_MAPS * tq
    seq = k_ref.shape[0]
    DIAG, SUB, FAR = 0, 1, 2

    @pl.when((b == 0) & (t == 0))
    def _():
        qm_scr[...] = jnp.zeros(qm_scr.shape, BF16)
        kk = lax.broadcasted_iota(jnp.int32, (tk, tq), 0)
        qq = lax.broadcasted_iota(jnp.int32, (tk, tq), 1)
        for h in range(N_HEADS):
            far = tbl_ref[h * REL_BUCKETS + REL_BUCKETS - 1]
            cfar_scr[:, 2 * h * tq:(2 * h + 2) * tq] = jnp.full((1, 2 * tq), far, F32)
            bias_scr[FAR, :, 2 * h * tq:(2 * h + 2) * tq] = jnp.full((tk, 2 * tq), far, F32)
            for off in (DIAG, SUB):
                d = qq - kk + off * tk
                val = jnp.full((tk, tq), tbl_ref[h * REL_BUCKETS], F32)
                for bk in range(1, REL_BUCKETS):
                    val = jnp.where(d >= starts[bk], tbl_ref[h * REL_BUCKETS + bk], val)
                if off == DIAG:
                    val = jnp.where(d >= 0, val, NEG_INF)
                for mp in range(2):
                    bias_scr[off, :, (2 * h + mp) * tq:(2 * h + mp + 1) * tq] = val

    @pl.when(t == 0)
    def _():
        for c0 in range(0, seq, tk):
            kf = k_ref[c0:c0 + tk, :].astype(F32)
            kn_scr[c0:c0 + tk, :] = (kf * _group_rms(kf, gsum_ref) * gk_ref[...]).astype(BF16)
            vt = v_ref[c0:c0 + tk, :].astype(F32).T.astype(BF16)
            for h in range(N_HEADS):
                vt_scr[h, 0:HEAD_V, c0:c0 + tk] = vt[h * HEAD_V:(h + 1) * HEAD_V, :]
                vt_scr[h, HEAD_V:V_ROWS, c0:c0 + tk] = jnp.ones((V_ROWS - HEAD_V, tk), BF16)

    for x, q_ref in enumerate((qa_ref, qb_ref)):
        qf = q_ref[...].astype(F32)
        qt = (qf * _group_rms(qf, gsum_ref) * gq_ref[...]).T
        qt = qt.astype(BF16)
        for g in range(N_MAPS):
            rows = slice(g * HEAD_QK, (g + 1) * HEAD_QK)
            qm_scr[x, rows, g * tq:(g + 1) * tq] = qt[rows, :]
    m_scr[...] = jnp.full(m_scr.shape, NEG_INF, F32)
    acc_scr[...] = jnp.zeros(acc_scr.shape, F32)

    def scores(n, x, j, bias_tile, constant_bias):
        sbuf = (s0_scr, s1_scr)[n % 2].at[jnp.minimum(t, 0)]
        c0 = pl.multiple_of(j * tk, tk)
        s = jnp.dot(kn_scr[pl.ds(c0, tk), :], qm_scr[x], preferred_element_type=F32)
        m_old = m_scr[x]
        if constant_bias:
            sbuf[...] = s
            c = cfar_scr[...]
            m_new = jnp.maximum(m_old, jnp.max(s, axis=0, keepdims=True) + c)
            shift = m_new - c
        else:
            s = s + bias_scr[bias_tile]
            sbuf[...] = s
            m_new = jnp.maximum(m_old, jnp.max(s, axis=0, keepdims=True))
            shift = m_new
        m_scr[x] = m_new
        return x, c0, sbuf, shift, jnp.exp2(m_old - m_new)

    def accumulate(x, c0, sbuf, shift, alpha):
        pb = jnp.exp2(sbuf[...] - shift).astype(BF16)
        for h in range(N_HEADS):
            w = slice(2 * h * tq, (2 * h + 2) * tq)
            pv = jnp.dot(vt_scr[h, :, pl.ds(c0, tk)], pb[:, w], preferred_element_type=F32)
            acc_scr[x, h] = alpha[:, w] * acc_scr[x, h] + pv

    qa = t
    qb = nq - 1 - t
    has_sub = t > 0
    n_far_a = jnp.maximum(t - 1, 0)
    blocks = [(0, qa, DIAG, False), (1, qb, DIAG, False), (1, qb - 1, SUB, False),
              (jnp.where(has_sub, 0, 1), jnp.where(has_sub, qa - 1, nq - 3), jnp.where(has_sub, SUB, FAR), False)]
    for i in range(nq - 3):
        to_a = i < n_far_a
        blocks.append((jnp.where(to_a, 0, 1), jnp.where(to_a, i, i - n_far_a), FAR, True))
    staged = scores(0, *blocks[0])
    for n in range(len(blocks)):
        following = scores(n + 1, *blocks[n + 1]) if n + 1 < len(blocks) else None
        accumulate(*staged)
        staged = following

    lp = lam_ref[...]
    lam = (jnp.exp(jnp.sum(lp[0:1] * lp[1:2], axis=-1, keepdims=True))
           - jnp.exp(jnp.sum(lp[2:3] * lp[3:4], axis=-1, keepdims=True)) + lam_init)
    for x in range(2):
        heads = []
        for h in range(N_HEADS):
            r = acc_scr[x, h, 0:HEAD_V, :] / acc_scr[x, h, HEAD_V:HEAD_V + 1, :]
            o_h = r[:, :tq] - lam * r[:, tq:]
            ms = jnp.mean(o_h * o_h, axis=0, keepdims=True)
            heads.append(o_h * lax.rsqrt(ms + EPS) * subg_ref[...] * (1.0 - lam_init))
        o_ref[x] = jnp.concatenate(heads, axis=0).T.astype(o_ref.dtype)


def _attention(l, qkv, tbl, gq, gk, gsum, lam_params, subg_col, batch, seq, tq, lam_init):
    t = qkv.shape[0]
    nq = seq // tq
    assert nq % 2 == 0 and nq >= 4
    nh = nq // 2
    starts = _bucket_starts(2 * tq)
    assert starts[REL_BUCKETS - 1] <= tq, "blocks two or more tiles below the diagonal must sit in the last bucket"
    whole = functools.partial(_layer_block, l)
    kern = functools.partial(_attn_kernel, tq=tq, nq=nq, lam_init=lam_init, starts=starts)
    return pl.pallas_call(
        kern,
        grid=(batch, nh),
        in_specs=[
            pl.BlockSpec(memory_space=pltpu.SMEM),
            pl.BlockSpec((tq, GROUP_W), lambda b, i: (b * nq + i, 0)),
            pl.BlockSpec((tq, GROUP_W), lambda b, i: (b * nq + nq - 1 - i, 0)),
            pl.BlockSpec((seq, GROUP_W), lambda b, i: (b, 1)),
            pl.BlockSpec((seq, GROUP_W), lambda b, i: (b, 2)),
            whole((1, GROUP_W)),
            whole((1, GROUP_W)),
            pl.BlockSpec((GROUP_W, GROUP_W), lambda b, i: (0, 0)),
            whole((4, HEAD_QK)),
            whole((HEAD_V, 1)),
        ],
        out_specs=pl.BlockSpec((2, tq, GROUP_W), lambda b, i: (0, b * nh + i, 0)),
        out_shape=jax.ShapeDtypeStruct((2, t // 2, GROUP_W), BF16),
        scratch_shapes=[
            pltpu.VMEM((seq, GROUP_W), BF16),
            pltpu.VMEM((N_HEADS, V_ROWS, seq), BF16),
            pltpu.VMEM((3, tq, N_MAPS * tq), F32),
            pltpu.VMEM((1, N_MAPS * tq), F32),
            pltpu.VMEM((2, GROUP_W, N_MAPS * tq), BF16),
            pltpu.VMEM((2, 1, N_MAPS * tq), F32),
            pltpu.VMEM((2, N_HEADS, V_ROWS, 2 * tq), F32),
            pltpu.VMEM((1, tq, N_MAPS * tq), F32),
            pltpu.VMEM((1, tq, N_MAPS * tq), F32),
        ],
        compiler_params=pltpu.CompilerParams(
            dimension_semantics=("arbitrary", "arbitrary"), vmem_limit_bytes=VMEM_LIMIT),
        name="diff_attention",
    )(tbl, qkv, qkv, qkv, qkv, gq, gk, gsum, lam_params, subg_col)


def _out_ffn_kernel(x_ref, mix_ref, att0_ref, att1_ref, wo_ref, g2_ref, wg_ref, wu_ref, wd_ref, o_ref, a_scr,
                    *, ff_chunk):
    mix_w = mix_ref.shape[1]
    att = jnp.concatenate([att0_ref[...], att1_ref[...]], axis=0)
    x1 = (x_ref[...]
          + jnp.dot(mix_ref[...], wo_ref[0:mix_w, :], preferred_element_type=F32)
          + jnp.dot(att, wo_ref[mix_w:, :], preferred_element_type=F32))
    ms = jnp.mean(x1 * x1, axis=-1, keepdims=True)
    h2 = (x1 * lax.rsqrt(ms + EPS) * g2_ref[...]).astype(BF16)
    for c0 in range(0, D_FF, ff_chunk):
        gate = jnp.dot(h2, wg_ref[:, c0:c0 + ff_chunk], preferred_element_type=F32)
        up = jnp.dot(h2, wu_ref[:, c0:c0 + ff_chunk], preferred_element_type=F32)
        a_scr[:, c0:c0 + ff_chunk] = (gate * jax.nn.sigmoid(gate) * up).astype(BF16)
    o_ref[...] = x1 + jnp.dot(a_scr[...], wd_ref[...], preferred_element_type=F32)


def _out_ffn(l, x2d, mix, att, wo, g2, wg, wu, wd, tm, seq, tq):
    t = x2d.shape[0]
    nq = seq // tq
    nh = nq // 2
    assert tm == 2 * tq
    tiles_per_seq = seq // tm
    whole = functools.partial(_layer_block, l)
    single = pl.Buffered(1)

    def att_block(which):
        def index(i):
            bb = i // tiles_per_seq
            u = 2 * (i % tiles_per_seq) + which
            lower = u < nh
            return (jnp.where(lower, 0, 1), bb * nh + jnp.where(lower, u, nq - 1 - u), 0)
        return pl.BlockSpec((None, tq, GROUP_W), index)

    kern = functools.partial(_out_ffn_kernel, ff_chunk=256)
    return pl.pallas_call(
        kern,
        grid=(t // tm,),
        in_specs=[
            pl.BlockSpec((tm, D_MODEL), lambda i: (i, 0)),
            pl.BlockSpec((tm, mix.shape[1]), lambda i: (i, 0)),
            att_block(0),
            att_block(1),
            whole((D_MODEL, D_MODEL), pipeline_mode=single),
            whole((1, D_MODEL)),
            whole((D_MODEL, D_FF), pipeline_mode=single),
            whole((D_MODEL, D_FF), pipeline_mode=single),
            whole((D_FF, D_MODEL), pipeline_mode=single),
        ],
        out_specs=pl.BlockSpec((tm, D_MODEL), lambda i: (i, 0)),
        out_shape=jax.ShapeDtypeStruct((t, D_MODEL), F32),
        scratch_shapes=[pltpu.VMEM((tm, D_FF), BF16)],
        compiler_params=pltpu.CompilerParams(
            dimension_semantics=("arbitrary",), vmem_limit_bytes=VMEM_LIMIT),
        name="out_ffn",
    )(x2d, mix, att, att, wo, g2, wg, wu, wd)


def _block_diag_pool(pool_w):
    out = jnp.zeros((pool_w.shape[0], GROUP_W, GROUP_W), pool_w.dtype)
    for g in range(len(POOL_WINDOWS)):
        out = out.at[:, g * POOL_GROUP:(g + 1) * POOL_GROUP, g * POOL_GROUP:(g + 1) * POOL_GROUP].set(pool_w[:, g])
    return out


def kernel(x, norm1_g, w_in, pool_w, pool_scale, sconv_w, conf_dw_w, conf_dw_b, conf_ln_g, conf_ln_b,
           q_norm_g, k_norm_g, lam_q1, lam_k1, lam_q2, lam_k2, subln_g, w_out, norm2_g, w_gate, w_up,
           w_down, rel_bias):
    batch, seq, d = x.shape
    assert d == D_MODEL and w_in.shape[-1] == D_IN
    depth = w_in.shape[0]
    t = batch * seq
    tq = 256
    tm = 2 * tq
    ts = 256
    assert seq % tm == 0 and seq % ts == 0 and ts % HALO == 0

    gid = np.arange(GROUP_W) // HEAD_QK
    gsum = jnp.asarray((gid[:, None] == gid[None, :]).astype(np.float32), BF16)
    tbl = (rel_bias.astype(F32).T * LOG2E).reshape(-1)
    rows = lambda v: v.astype(F32)[:, None, :]
    rep8 = lambda v: jnp.broadcast_to(v.astype(F32)[..., None, :], v.shape[:-1] + (8, v.shape[-1]))
    in_mix_params = (rows(norm1_g), w_in.astype(BF16), _block_diag_pool(pool_w).astype(BF16), rows(pool_scale),
                     sconv_w.astype(F32), rep8(conf_dw_w), rep8(conf_dw_b), rep8(conf_ln_g), rep8(conf_ln_b))
    gq = rows(jnp.tile(q_norm_g.astype(F32), (1, N_MAPS)) * (HEAD_QK ** -0.5 * LOG2E))
    gk = rows(jnp.tile(k_norm_g.astype(F32), (1, N_MAPS)))
    lam_params = jnp.stack([lam_q1, lam_k1, lam_q2, lam_k2], axis=1).astype(F32)
    subg_col = subln_g.astype(F32)[:, :, None]
    ffn_params = (w_out.astype(BF16), rows(norm2_g), w_gate.astype(BF16), w_up.astype(BF16), w_down.astype(BF16))

    x2d = x.reshape(t, D_MODEL)
    for l in range(depth):
        lam_init = 0.8 - 0.6 * math.exp(-0.3 * l)
        qkv, mix = _in_mix(l, x2d, *in_mix_params, seq, ts)
        att = _attention(l, qkv, tbl, gq, gk, gsum, lam_params, subg_col, batch, seq, tq, lam_init)
        x2d = _out_ffn(l, x2d, mix, att, *ffn_params, tm, seq, tq)
    return x2d.reshape(batch, seq, D_MODEL)
```

```python
import functools
import math

import numpy as np
import jax
import jax.numpy as jnp
from jax import lax
from jax.experimental import pallas as pl
from jax.experimental.pallas import tpu as pltpu

F32 = jnp.float32
BF16 = jnp.bfloat16

D_MODEL = 1024
GROUP_W = 256
N_SLICES = 9
D_IN = N_SLICES * GROUP_W
N_MIX_SLICES = 6
POOL_WINDOWS = (2, 4, 8, 16)
POOL_GROUP = GROUP_W // len(POOL_WINDOWS)
SCONV_K = 3
CONF_K = 31
N_HEADS = 4
HEAD_V = GROUP_W // N_HEADS
HEAD_QK = HEAD_V // 2
V_ROWS = HEAD_V + 16
N_MAPS = 2 * N_HEADS
D_FF = 2816
REL_BUCKETS = 32
REL_MAX_EXACT = 16
REL_MAX_DIST = 128
EPS = 1e-6
NEG_INF = -1e30
LOG2E = math.log2(math.e)

HALO = 32
VMEM_LIMIT = 56 * 1024 * 1024


def _bucket_starts(max_dist):
    n = np.arange(0, max_dist + 1)
    nf = np.maximum(n, 1).astype(np.float32)
    large = REL_MAX_EXACT + (np.log(nf / np.float32(REL_MAX_EXACT)) / np.float32(math.log(REL_MAX_DIST / REL_MAX_EXACT))
                             * np.float32(REL_BUCKETS - REL_MAX_EXACT)).astype(np.int32)
    large = np.minimum(large, REL_BUCKETS - 1)
    bucket = np.where(n < REL_MAX_EXACT, n, large)
    starts = [int(np.argmax(bucket >= b)) for b in range(REL_BUCKETS)]
    assert bucket[-1] == REL_BUCKETS - 1
    return starts


def _layer_block(l, shape, **kwargs):
    index = (l,) + (0,) * len(shape)
    return pl.BlockSpec((None,) + tuple(shape), lambda *g: index, **kwargs)


def _in_mix_kernel(x_ref, g1_ref, w_ref, poolw_ref, pscale_ref, sconvw_ref, cw_ref, cb_ref, lng_ref, lnb_ref,
                   qkv_ref, mix_ref, zbuf, a0, b1, b2, b3, p_ext, u_ext, u_sh, h_scr, *, ts, nst, row_chunk):
    n = HALO + ts
    W = GROUP_W
    g = pl.program_id(0)
    sm = lax.rem(g + nst - 1, nst)

    @pl.when(g == 0)
    def _():
        zbuf[...] = jnp.zeros(zbuf.shape, F32)

    @pl.when(sm == 0)
    def _():
        a0[0:HALO, :] = jnp.zeros((HALO, W), F32)
        p_ext[0:8, :] = jnp.zeros((8, W), F32)
        u_ext[0:HALO, :] = jnp.zeros((HALO, W), F32)

    a0[HALO:n, :] = zbuf[:, 0:W]
    p_ext[8:8 + ts, :] = zbuf[:, 2 * W:3 * W] * zbuf[:, 3 * W:4 * W]
    conv3 = sconvw_ref[SCONV_K - 1:SCONV_K, :] * p_ext[8:8 + ts, :]
    for j in range(1, SCONV_K):
        conv3 = conv3 + sconvw_ref[SCONV_K - 1 - j:SCONV_K - j, :] * p_ext[8 - j:8 - j + ts, :]
    mix_ref[:, W:2 * W] = (zbuf[:, W:2 * W] * conv3).astype(mix_ref.dtype)
    p_ext[0:8, :] = p_ext[ts:ts + 8, :]
    u_ext[HALO:n, :] = zbuf[:, 4 * W:5 * W] * jax.nn.sigmoid(zbuf[:, 5 * W:6 * W])
    for r in range(1, 8):
        u_sh[r, 8:n, :] = u_ext[8 - r:n - r, :]

    x = x_ref[...]
    ms = jnp.mean(x * x, axis=-1, keepdims=True)
    h = h_scr.at[jnp.minimum(g, 0)]
    h[...] = (x_ref[...] * lax.rsqrt(ms + EPS) * g1_ref[...]).astype(BF16)

    def project(k):
        zk = jnp.dot(h[...], w_ref[:, k * W:(k + 1) * W], preferred_element_type=F32)
        if k < N_MIX_SLICES:
            zbuf[:, k * W:(k + 1) * W] = zk
        else:
            qkv_ref[:, (k - N_MIX_SLICES) * W:(k - N_MIX_SLICES + 1) * W] = zk.astype(qkv_ref.dtype)

    def rows8(v):
        return v.reshape(v.shape[0] // 8, 8, W)

    chunks = list(range(0, ts, row_chunk))
    assert len(chunks) >= N_SLICES - 1
    for ci, r0 in enumerate(chunks):
        if ci < N_SLICES - 1:
            project(ci)
        acc = cb_ref[...] + cw_ref[CONF_K - 1] * rows8(u_ext[HALO + r0:HALO + r0 + row_chunk, :])
        for j in range(1, CONF_K):
            q8, r = 8 * (j // 8), j % 8
            lo = HALO + r0 - q8
            src = u_ext[lo:lo + row_chunk, :] if r == 0 else u_sh[r, lo:lo + row_chunk, :]
            acc = acc + cw_ref[CONF_K - 1 - j] * rows8(src)
        mu = jnp.mean(acc, axis=-1, keepdims=True)
        xc = acc - mu
        var = jnp.mean(xc * xc, axis=-1, keepdims=True)
        yn = xc * lax.rsqrt(var + EPS) * lng_ref[...] + lnb_ref[...]
        mix_ref[r0:r0 + row_chunk, 2 * W:3 * W] = (
            (yn * jax.nn.sigmoid(yn)).reshape(row_chunk, W).astype(mix_ref.dtype))
    u_ext[0:HALO, :] = u_ext[ts:n, :]
    project(N_SLICES - 1)

    b1[8:n, :] = a0[8:n, :] + a0[7:n - 1, :]
    b2[16:n, :] = b1[16:n, :] + b1[14:n - 2, :]
    b3[24:n, :] = b2[24:n, :] + b2[20:n - 4, :]
    s16 = b3[HALO:n, :] + b3[HALO - 8:n - 8, :]
    lane = lax.broadcasted_iota(jnp.int32, (ts, W), 1)
    pos1 = sm * ts + lax.broadcasted_iota(jnp.int32, (ts, W), 0) + 1
    sel = jnp.where(lane < POOL_GROUP, b1[HALO:n, :],
                    jnp.where(lane < 2 * POOL_GROUP, b2[HALO:n, :],
                              jnp.where(lane < 3 * POOL_GROUP, b3[HALO:n, :], s16)))
    win = jnp.where(lane < POOL_GROUP, POOL_WINDOWS[0],
                    jnp.where(lane < 2 * POOL_GROUP, POOL_WINDOWS[1],
                              jnp.where(lane < 3 * POOL_GROUP, POOL_WINDOWS[2], POOL_WINDOWS[3])))
    cnt = jnp.minimum(pos1, win).astype(F32)
    pooled = sel / cnt - a0[HALO:n, :]
    y_pool = jnp.dot(pooled.astype(BF16), poolw_ref[...], preferred_element_type=F32) * pscale_ref[...]
    mix_ref[:, 0:W] = y_pool.astype(mix_ref.dtype)
    a0[0:HALO, :] = a0[ts:n, :]


def _in_mix(l, x2d, g1, w, poolw_bd, pscale, sconvw, cw, cb, lng, lnb, seq, ts):
    t = x2d.shape[0]
    nst = seq // ts
    n_tiles = t // ts
    whole = functools.partial(_layer_block, l)
    kern = functools.partial(_in_mix_kernel, ts=ts, nst=nst, row_chunk=32)
    out_w = 3 * GROUP_W
    return pl.pallas_call(
        kern,
        grid=(n_tiles + 1,),
        in_specs=[
            pl.BlockSpec((ts, D_MODEL), lambda g: (jnp.minimum(g, n_tiles - 1), 0)),
            whole((1, D_MODEL)),
            whole((D_MODEL, D_IN), pipeline_mode=pl.Buffered(1)),
            whole((GROUP_W, GROUP_W)),
            whole((1, GROUP_W)),
            whole((SCONV_K, GROUP_W)),
            whole((CONF_K, 8, GROUP_W)),
            whole((8, GROUP_W)),
            whole((8, GROUP_W)),
            whole((8, GROUP_W)),
        ],
        out_specs=[pl.BlockSpec((ts, out_w), lambda g: (jnp.minimum(g, n_tiles - 1), 0)),
                   pl.BlockSpec((ts, out_w), lambda g: (jnp.maximum(g - 1, 0), 0))],
        out_shape=[jax.ShapeDtypeStruct((t, out_w), BF16), jax.ShapeDtypeStruct((t, out_w), BF16)],
        scratch_shapes=[
            pltpu.VMEM((ts, N_MIX_SLICES * GROUP_W), F32),
            pltpu.VMEM((HALO + ts, GROUP_W), F32),
            pltpu.VMEM((HALO + ts, GROUP_W), F32),
            pltpu.VMEM((HALO + ts, GROUP_W), F32),
            pltpu.VMEM((HALO + ts, GROUP_W), F32),
            pltpu.VMEM((8 + ts, GROUP_W), F32),
            pltpu.VMEM((HALO + ts, GROUP_W), F32),
            pltpu.VMEM((8, HALO + ts, GROUP_W), F32),
            pltpu.VMEM((1, ts, D_MODEL), BF16),
        ],
        compiler_params=pltpu.CompilerParams(
            dimension_semantics=("arbitrary",), vmem_limit_bytes=VMEM_LIMIT),
        name="in_mix",
    )(x2d, g1, w, poolw_bd, pscale, sconvw, cw, cb, lng, lnb)


def _group_rms(x, gsum_ref):
    x2 = x * x
    hi = x2.astype(BF16)
    lo = (x2 - hi.astype(F32)).astype(BF16)
    ssq = (jnp.dot(hi, gsum_ref[...], preferred_element_type=F32)
           + jnp.dot(lo, gsum_ref[...], preferred_element_type=F32))
    return lax.rsqrt(ssq * (1.0 / HEAD_QK) + EPS)


def _attn_kernel(tbl_ref, qa_ref, qb_ref, k_ref, v_ref, gq_ref, gk_ref, gsum_ref, lam_ref, subg_ref,
                 o_ref, kn_scr, vt_scr, bias_scr, cfar_scr, qm_scr, m_scr, acc_scr, s0_scr, s1_scr,
                 *, tq, nq, lam_init, starts):
    b = pl.program_id(0)
    t = pl.program_id(1)
    tk = tq
    nw = N_MAPS * tq
    seq = k_ref.shape[0]
    DIAG, SUB, FAR = 0, 1, 2

    @pl.when((b == 0) & (t == 0))
    def _():
        qm_scr[...] = jnp.zeros(qm_scr.shape, BF16)
        kk = lax.broadcasted_iota(jnp.int32, (tk, tq), 0)
        qq = lax.broadcasted_iota(jnp.int32, (tk, tq), 1)
        for h in range(N_HEADS):
            far = tbl_ref[h * REL_BUCKETS + REL_BUCKETS - 1]
            cfar_scr[:, 2 * h * tq:(2 * h + 2) * tq] = jnp.full((1, 2 * tq), far, F32)
            bias_scr[FAR, :, 2 * h * tq:(2 * h + 2) * tq] = jnp.full((tk, 2 * tq), far, F32)
            for off in (DIAG, SUB):
                d = qq - kk + off * tk
                val = jnp.full((tk, tq), tbl_ref[h * REL_BUCKETS], F32)
                for bk in range(1, REL_BUCKETS):
                    val = jnp.where(d >= starts[bk], tbl_ref[h * REL_BUCKETS + bk], val)
                if off == DIAG:
                    val = jnp.where(d >= 0, val, NEG_INF)
                for mp in range(2):
                    bias_scr[off, :, (2 * h + mp) * tq:(2 * h + mp + 1) * tq] = val

    @pl.when(t == 0)
    def _():
        for c0 in range(0, seq, tk):
            kf = k_ref[c0:c0 + tk, :].astype(F32)
            kn_scr[c0:c0 + tk, :] = (kf * _group_rms(kf, gsum_ref) * gk_ref[...]).astype(BF16)
            vt = v_ref[c0:c0 + tk, :].astype(F32).T.astype(BF16)
            for h in range(N_HEADS):
                vt_scr[h, 0:HEAD_V, c0:c0 + tk] = vt[h * HEAD_V:(h + 1) * HEAD_V, :]
                vt_scr[h, HEAD_V:V_ROWS, c0:c0 + tk] = jnp.ones((V_ROWS - HEAD_V, tk), BF16)

    for x, q_ref in enumerate((qa_ref, qb_ref)):
        qf = q_ref[...].astype(F32)
        qt = (qf * _group_rms(qf, gsum_ref) * gq_ref[...]).T
        qt = qt.astype(BF16)
        for g in range(N_MAPS):
            rows = slice(g * HEAD_QK, (g + 1) * HEAD_QK)
            qm_scr[x, rows, g * tq:(g + 1) * tq] = qt[rows, :]
    m_scr[...] = jnp.full(m_scr.shape, NEG_INF, F32)
    acc_scr[...] = jnp.zeros(acc_scr.shape, F32)

    def scores(n, x, j, bias_tile, constant_bias):
        sbuf = (s0_scr, s1_scr)[n % 2].at[jnp.minimum(t, 0)]
        c0 = pl.multiple_of(j * tk, tk)
        s = jnp.dot(kn_scr[pl.ds(c0, tk), :], qm_scr[x], preferred_element_type=F32)
        m_old = m_scr[x]
        if constant_bias:
            sbuf[...] = s
            c = cfar_scr[...]
            m_new = jnp.maximum(m_old, jnp.max(s, axis=0, keepdims=True) + c)
            shift = m_new - c
        else:
            s = s + bias_scr[bias_tile]
            sbuf[...] = s
            m_new = jnp.maximum(m_old, jnp.max(s, axis=0, keepdims=True))
            shift = m_new
        m_scr[x] = m_new
        return x, c0, sbuf, shift, jnp.exp2(m_old - m_new)

    def accumulate(x, c0, sbuf, shift, alpha):
        pb = jnp.exp2(sbuf[...] - shift).astype(BF16)
        for h in range(N_HEADS):
            w = slice(2 * h * tq, (2 * h + 2) * tq)
            pv = jnp.dot(vt_scr[h, :, pl.ds(c0, tk)], pb[:, w], preferred_element_type=F32)
            acc_scr[x, h] = alpha[:, w] * acc_scr[x, h] + pv

    qa = t
    qb = nq - 1 - t
    has_sub = t > 0
    n_far_a = jnp.maximum(t - 1, 0)
    blocks = [(0, qa, DIAG, False), (1, qb, DIAG, False), (1, qb - 1, SUB, False),
              (jnp.where(has_sub, 0, 1), jnp.where(has_sub, qa - 1, nq - 3), jnp.where(has_sub, SUB, FAR), False)]
    for i in range(nq - 3):
        to_a = i < n_far_a
        blocks.append((jnp.where(to_a, 0, 1), jnp.where(to_a, i, i - n_far_a), FAR, True))
    staged = scores(0, *blocks[0])
    for n in range(len(blocks)):
        following = scores(n + 1, *blocks[n + 1]) if n + 1 < len(blocks) else None
        accumulate(*staged)
        staged = following

    lp = lam_ref[...]
    lam = (jnp.exp(jnp.sum(lp[0:1] * lp[1:2], axis=-1, keepdims=True))
           - jnp.exp(jnp.sum(lp[2:3] * lp[3:4], axis=-1, keepdims=True)) + lam_init)
    for x in range(2):
        heads = []
        for h in range(N_HEADS):
            r = acc_scr[x, h, 0:HEAD_V, :] / acc_scr[x, h, HEAD_V:HEAD_V + 1, :]
            o_h = r[:, :tq] - lam * r[:, tq:]
            ms = jnp.mean(o_h * o_h, axis=0, keepdims=True)
            heads.append(o_h * lax.rsqrt(ms + EPS) * subg_ref[...] * (1.0 - lam_init))
        o_ref[x] = jnp.concatenate(heads, axis=0).T.astype(o_ref.dtype)


def _attention(l, qkv, tbl, gq, gk, gsum, lam_params, subg_col, batch, seq, tq, lam_init):
    t = qkv.shape[0]
    nq = seq // tq
    assert nq % 2 == 0 and nq >= 4
    nh = nq // 2
    starts = _bucket_starts(2 * tq)
    assert starts[REL_BUCKETS - 1] <= tq, "blocks two or more tiles below the diagonal must sit in the last bucket"
    whole = functools.partial(_layer_block, l)
    kern = functools.partial(_attn_kernel, tq=tq, nq=nq, lam_init=lam_init, starts=starts)
    return pl.pallas_call(
        kern,
        grid=(batch, nh),
        in_specs=[
            pl.BlockSpec(memory_space=pltpu.SMEM),
            pl.BlockSpec((tq, GROUP_W), lambda b, i: (b * nq + i, 0)),
            pl.BlockSpec((tq, GROUP_W), lambda b, i: (b * nq + nq - 1 - i, 0)),
            pl.BlockSpec((seq, GROUP_W), lambda b, i: (b, 1)),
            pl.BlockSpec((seq, GROUP_W), lambda b, i: (b, 2)),
            whole((1, GROUP_W)),
            whole((1, GROUP_W)),
            pl.BlockSpec((GROUP_W, GROUP_W), lambda b, i: (0, 0)),
            whole((4, HEAD_QK)),
            whole((HEAD_V, 1)),
        ],
        out_specs=pl.BlockSpec((2, tq, GROUP_W), lambda b, i: (0, b * nh + i, 0)),
        out_shape=jax.ShapeDtypeStruct((2, t // 2, GROUP_W), BF16),
        scratch_shapes=[
            pltpu.VMEM((seq, GROUP_W), BF16),
            pltpu.VMEM((N_HEADS, V_ROWS, seq), BF16),
            pltpu.VMEM((3, tq, N_MAPS * tq), F32),
            pltpu.VMEM((1, N_MAPS * tq), F32),
            pltpu.VMEM((2, GROUP_W, N_MAPS * tq), BF16),
            pltpu.VMEM((2, 1, N_MAPS * tq), F32),
            pltpu.VMEM((2, N_HEADS, V_ROWS, 2 * tq), F32),
            pltpu.VMEM((1, tq, N_MAPS * tq), F32),
            pltpu.VMEM((1, tq, N_MAPS * tq), F32),
        ],
        compiler_params=pltpu.CompilerParams(
            dimension_semantics=("arbitrary", "arbitrary"), vmem_limit_bytes=VMEM_LIMIT),
        name="diff_attention",
    )(tbl, qkv, qkv, qkv, qkv, gq, gk, gsum, lam_params, subg_col)


def _out_ffn_kernel(x_ref, mix_ref, att0_ref, att1_ref, wo_ref, g2_ref, wg_ref, wu_ref, wd_ref, o_ref, a_scr,
                    *, ff_chunk):
    mix_w = mix_ref.shape[1]
    att = jnp.concatenate([att0_ref[...], att1_ref[...]], axis=0)
    x1 = (x_ref[...]
          + jnp.dot(mix_ref[...], wo_ref[0:mix_w, :], preferred_element_type=F32)
          + jnp.dot(att, wo_ref[mix_w:, :], preferred_element_type=F32))
    ms = jnp.mean(x1 * x1, axis=-1, keepdims=True)
    h2 = (x1 * lax.rsqrt(ms + EPS) * g2_ref[...]).astype(BF16)
    for c0 in range(0, D_FF, ff_chunk):
        gate = jnp.dot(h2, wg_ref[:, c0:c0 + ff_chunk], preferred_element_type=F32)
        up = jnp.dot(h2, wu_ref[:, c0:c0 + ff_chunk], preferred_element_type=F32)
        a_scr[:, c0:c0 + ff_chunk] = (gate * jax.nn.sigmoid(gate) * up).astype(BF16)
    o_ref[...] = x1 + jnp.dot(a_scr[...], wd_ref[...], preferred_element_type=F32)


def _out_ffn(l, x2d, mix, att, wo, g2, wg, wu, wd, tm, seq, tq):
    t = x2d.shape[0]
    nq = seq // tq
    nh = nq // 2
    assert tm == 2 * tq
    tiles_per_seq = seq // tm
    whole = functools.partial(_layer_block, l)
    single = pl.Buffered(1)

    def att_block(which):
        def index(i):
            bb = i // tiles_per_seq
            u = 2 * (i % tiles_per_seq) + which
            lower = u < nh
            return (jnp.where(lower, 0, 1), bb * nh + jnp.where(lower, u, nq - 1 - u), 0)
        return pl.BlockSpec((None, tq, GROUP_W), index)

    kern = functools.partial(_out_ffn_kernel, ff_chunk=256)
    return pl.pallas_call(
        kern,
        grid=(t // tm,),
        in_specs=[
            pl.BlockSpec((tm, D_MODEL), lambda i: (i, 0)),
            pl.BlockSpec((tm, mix.shape[1]), lambda i: (i, 0)),
            att_block(0),
            att_block(1),
            whole((D_MODEL, D_MODEL), pipeline_mode=single),
            whole((1, D_MODEL)),
            whole((D_MODEL, D_FF), pipeline_mode=single),
            whole((D_MODEL, D_FF), pipeline_mode=single),
            whole((D_FF, D_MODEL), pipeline_mode=single),
        ],
        out_specs=pl.BlockSpec((tm, D_MODEL), lambda i: (i, 0)),
        out_shape=jax.ShapeDtypeStruct((t, D_MODEL), F32),
        scratch_shapes=[pltpu.VMEM((tm, D_FF), BF16)],
        compiler_params=pltpu.CompilerParams(
            dimension_semantics=("arbitrary",), vmem_limit_bytes=VMEM_LIMIT),
        name="out_ffn",
    )(x2d, mix, att, att, wo, g2, wg, wu, wd)


def _block_diag_pool(pool_w):
    out = jnp.zeros((pool_w.shape[0], GROUP_W, GROUP_W), pool_w.dtype)
    for g in range(len(POOL_WINDOWS)):
        out = out.at[:, g * POOL_GROUP:(g + 1) * POOL_GROUP, g * POOL_GROUP:(g + 1) * POOL_GROUP].set(pool_w[:, g])
    return out


def kernel(x, norm1_g, w_in, pool_w, pool_scale, sconv_w, conf_dw_w, conf_dw_b, conf_ln_g, conf_ln_b,
           q_norm_g, k_norm_g, lam_q1, lam_k1, lam_q2, lam_k2, subln_g, w_out, norm2_g, w_gate, w_up,
           w_down, rel_bias):
    batch, seq, d = x.shape
    assert d == D_MODEL and w_in.shape[-1] == D_IN
    depth = w_in.shape[0]
    t = batch * seq
    tq = 256
    tm = 2 * tq
    ts = 512
    assert seq % tm == 0 and seq % ts == 0 and ts % HALO == 0

    gid = np.arange(GROUP_W) // HEAD_QK
    gsum = jnp.asarray((gid[:, None] == gid[None, :]).astype(np.float32), BF16)
    tbl = (rel_bias.astype(F32).T * LOG2E).reshape(-1)
    rows = lambda v: v.astype(F32)[:, None, :]
    rep8 = lambda v: jnp.broadcast_to(v.astype(F32)[..., None, :], v.shape[:-1] + (8, v.shape[-1]))
    in_mix_params = (rows(norm1_g), w_in.astype(BF16), _block_diag_pool(pool_w).astype(BF16), rows(pool_scale),
                     sconv_w.astype(F32), rep8(conf_dw_w), rep8(conf_dw_b), rep8(conf_ln_g), rep8(conf_ln_b))
    gq = rows(jnp.tile(q_norm_g.astype(F32), (1, N_MAPS)) * (HEAD_QK ** -0.5 * LOG2E))
    gk = rows(jnp.tile(k_norm_g.astype(F32), (1, N_MAPS)))
    lam_params = jnp.stack([lam_q1, lam_k1, lam_q2, lam_k2], axis=1).astype(F32)
    subg_col = subln_g.astype(F32)[:, :, None]
    ffn_params = (w_out.astype(BF16), rows(norm2_g), w_gate.astype(BF16), w_up.astype(BF16), w_down.astype(BF16))

    x2d = x.reshape(t, D_MODEL)
    for l in range(depth):
        lam_init = 0.8 - 0.6 * math.exp(-0.3 * l)
        qkv, mix = _in_mix(l, x2d, *in_mix_params, seq, ts)
        att = _attention(l, qkv, tbl, gq, gk, gsum, lam_params, subg_col, batch, seq, tq, lam_init)
        x2d = _out_ffn(l, x2d, mix, att, *ffn_params, tm, seq, tq)
    return x2d.reshape(batch, seq, D_MODEL)
```

```python
import functools
import math

import numpy as np
import jax
import jax.numpy as jnp
from jax import lax
from jax.experimental import pallas as pl
from jax.experimental.pallas import tpu as pltpu

F32 = jnp.float32
BF16 = jnp.bfloat16

D_MODEL = 1024
GROUP_W = 256
N_SLICES = 9
D_IN = N_SLICES * GROUP_W
N_MIX_SLICES = 6
POOL_WINDOWS = (2, 4, 8, 16)
POOL_GROUP = GROUP_W // len(POOL_WINDOWS)
SCONV_K = 3
CONF_K = 31
N_HEADS = 4
HEAD_V = GROUP_W // N_HEADS
HEAD_QK = HEAD_V // 2
V_ROWS = HEAD_V + 16
N_MAPS = 2 * N_HEADS
D_FF = 2816
REL_BUCKETS = 32
REL_MAX_EXACT = 16
REL_MAX_DIST = 128
EPS = 1e-6
NEG_INF = -1e30
LOG2E = math.log2(math.e)

HALO = 32
VMEM_LIMIT = 56 * 1024 * 1024


def _bucket_starts(max_dist):
    n = np.arange(0, max_dist + 1)
    nf = np.maximum(n, 1).astype(np.float32)
    large = REL_MAX_EXACT + (np.log(nf / np.float32(REL_MAX_EXACT)) / np.float32(math.log(REL_MAX_DIST / REL_MAX_EXACT))
                             * np.float32(REL_BUCKETS - REL_MAX_EXACT)).astype(np.int32)
    large = np.minimum(large, REL_BUCKETS - 1)
    bucket = np.where(n < REL_MAX_EXACT, n, large)
    starts = [int(np.argmax(bucket >= b)) for b in range(REL_BUCKETS)]
    assert bucket[-1] == REL_BUCKETS - 1
    return starts


def _layer_block(l, shape, **kwargs):
    index = (l,) + (0,) * len(shape)
    return pl.BlockSpec((None,) + tuple(shape), lambda *g: index, **kwargs)


def _in_mix_kernel(x_ref, g1_ref, w_ref, poolw_ref, pscale_ref, sconvw_ref, cw_ref, cb_ref, lng_ref, lnb_ref,
                   qkv_ref, mix_ref, zbuf, a0, b1, b2, b3, p_ext, u_ext, u_sh, h_scr, *, ts, nst, row_chunk):
    n = HALO + ts
    W = GROUP_W
    g = pl.program_id(0)
    sm = lax.rem(g + nst - 1, nst)

    @pl.when(g == 0)
    def _():
        zbuf[...] = jnp.zeros(zbuf.shape, F32)

    @pl.when(sm == 0)
    def _():
        a0[0:HALO, :] = jnp.zeros((HALO, W), F32)
        p_ext[0:8, :] = jnp.zeros((8, W), F32)
        u_ext[0:HALO, :] = jnp.zeros((HALO, W), F32)

    a0[HALO:n, :] = zbuf[:, 0:W]
    p_ext[8:8 + ts, :] = zbuf[:, 2 * W:3 * W] * zbuf[:, 3 * W:4 * W]
    conv3 = sconvw_ref[SCONV_K - 1:SCONV_K, :] * p_ext[8:8 + ts, :]
    for j in range(1, SCONV_K):
        conv3 = conv3 + sconvw_ref[SCONV_K - 1 - j:SCONV_K - j, :] * p_ext[8 - j:8 - j + ts, :]
    mix_ref[:, W:2 * W] = (zbuf[:, W:2 * W] * conv3).astype(mix_ref.dtype)
    p_ext[0:8, :] = p_ext[ts:ts + 8, :]
    u_ext[HALO:n, :] = zbuf[:, 4 * W:5 * W] * jax.nn.sigmoid(zbuf[:, 5 * W:6 * W])
    for r in range(1, 8):
        u_sh[r, 8:n, :] = u_ext[8 - r:n - r, :]

    x = x_ref[...]
    ms = jnp.mean(x * x, axis=-1, keepdims=True)
    h = h_scr.at[jnp.minimum(g, 0)]
    h[...] = (x_ref[...] * lax.rsqrt(ms + EPS) * g1_ref[...]).astype(BF16)

    def project(k):
        zk = jnp.dot(h[...], w_ref[:, k * W:(k + 1) * W], preferred_element_type=F32)
        if k < N_MIX_SLICES:
            zbuf[:, k * W:(k + 1) * W] = zk
        else:
            qkv_ref[:, (k - N_MIX_SLICES) * W:(k - N_MIX_SLICES + 1) * W] = zk.astype(qkv_ref.dtype)

    def rows8(v):
        return v.reshape(v.shape[0] // 8, 8, W)

    chunks = list(range(0, ts, row_chunk))
    assert len(chunks) >= N_SLICES - 1
    for ci, r0 in enumerate(chunks):
        if ci < N_SLICES - 1:
            project(ci)
        acc = cb_ref[...] + cw_ref[CONF_K - 1] * rows8(u_ext[HALO + r0:HALO + r0 + row_chunk, :])
        for j in range(1, CONF_K):
            q8, r = 8 * (j // 8), j % 8
            lo = HALO + r0 - q8
            src = u_ext[lo:lo + row_chunk, :] if r == 0 else u_sh[r, lo:lo + row_chunk, :]
            acc = acc + cw_ref[CONF_K - 1 - j] * rows8(src)
        mu = jnp.mean(acc, axis=-1, keepdims=True)
        xc = acc - mu
        var = jnp.mean(xc * xc, axis=-1, keepdims=True)
        yn = xc * lax.rsqrt(var + EPS) * lng_ref[...] + lnb_ref[...]
        mix_ref[r0:r0 + row_chunk, 2 * W:3 * W] = (
            (yn * jax.nn.sigmoid(yn)).reshape(row_chunk, W).astype(mix_ref.dtype))
    u_ext[0:HALO, :] = u_ext[ts:n, :]
    project(N_SLICES - 1)

    b1[8:n, :] = a0[8:n, :] + a0[7:n - 1, :]
    b2[16:n, :] = b1[16:n, :] + b1[14:n - 2, :]
    b3[24:n, :] = b2[24:n, :] + b2[20:n - 4, :]
    s16 = b3[HALO:n, :] + b3[HALO - 8:n - 8, :]
    lane = lax.broadcasted_iota(jnp.int32, (ts, W), 1)
    pos1 = sm * ts + lax.broadcasted_iota(jnp.int32, (ts, W), 0) + 1
    sel = jnp.where(lane < POOL_GROUP, b1[HALO:n, :],
                    jnp.where(lane < 2 * POOL_GROUP, b2[HALO:n, :],
                              jnp.where(lane < 3 * POOL_GROUP, b3[HALO:n, :], s16)))
    win = jnp.where(lane < POOL_GROUP, POOL_WINDOWS[0],
                    jnp.where(lane < 2 * POOL_GROUP, POOL_WINDOWS[1],
                              jnp.where(lane < 3 * POOL_GROUP, POOL_WINDOWS[2], POOL_WINDOWS[3])))
    cnt = jnp.minimum(pos1, win).astype(F32)
    pooled = sel / cnt - a0[HALO:n, :]
    y_pool = jnp.dot(pooled.astype(BF16), poolw_ref[...], preferred_element_type=F32) * pscale_ref[...]
    mix_ref[:, 0:W] = y_pool.astype(mix_ref.dtype)
    a0[0:HALO, :] = a0[ts:n, :]


def _in_mix(l, x2d, g1, w, poolw_bd, pscale, sconvw, cw, cb, lng, lnb, seq, ts):
    t = x2d.shape[0]
    nst = seq // ts
    n_tiles = t // ts
    whole = functools.partial(_layer_block, l)
    kern = functools.partial(_in_mix_kernel, ts=ts, nst=nst, row_chunk=32)
    out_w = 3 * GROUP_W
    return pl.pallas_call(
        kern,
        grid=(n_tiles + 1,),
        in_specs=[
            pl.BlockSpec((ts, D_MODEL), lambda g: (jnp.minimum(g, n_tiles - 1), 0)),
            whole((1, D_MODEL)),
            whole((D_MODEL, D_IN), pipeline_mode=pl.Buffered(1)),
            whole((GROUP_W, GROUP_W)),
            whole((1, GROUP_W)),
            whole((SCONV_K, GROUP_W)),
            whole((CONF_K, 8, GROUP_W)),
            whole((8, GROUP_W)),
            whole((8, GROUP_W)),
            whole((8, GROUP_W)),
        ],
        out_specs=[pl.BlockSpec((ts, out_w), lambda g: (jnp.minimum(g, n_tiles - 1), 0)),
                   pl.BlockSpec((ts, out_w), lambda g: (jnp.maximum(g - 1, 0), 0))],
        out_shape=[jax.ShapeDtypeStruct((t, out_w), BF16), jax.ShapeDtypeStruct((t, out_w), BF16)],
        scratch_shapes=[
            pltpu.VMEM((ts, N_MIX_SLICES * GROUP_W), F32),
            pltpu.VMEM((HALO + ts, GROUP_W), F32),
            pltpu.VMEM((HALO + ts, GROUP_W), F32),
            pltpu.VMEM((HALO + ts, GROUP_W), F32),
            pltpu.VMEM((HALO + ts, GROUP_W), F32),
            pltpu.VMEM((8 + ts, GROUP_W), F32),
            pltpu.VMEM((HALO + ts, GROUP_W), F32),
            pltpu.VMEM((8, HALO + ts, GROUP_W), F32),
            pltpu.VMEM((1, ts, D_MODEL), BF16),
        ],
        compiler_params=pltpu.CompilerParams(
            dimension_semantics=("arbitrary",), vmem_limit_bytes=VMEM_LIMIT),
        name="in_mix",
    )(x2d, g1, w, poolw_bd, pscale, sconvw, cw, cb, lng, lnb)


def _group_rms(x, gsum_ref):
    x2 = x * x
    hi = x2.astype(BF16)
    lo = (x2 - hi.astype(F32)).astype(BF16)
    ssq = (jnp.dot(hi, gsum_ref[...], preferred_element_type=F32)
           + jnp.dot(lo, gsum_ref[...], preferred_element_type=F32))
    return lax.rsqrt(ssq * (1.0 / HEAD_QK) + EPS)


def _attn_kernel(tbl_ref, qa_ref, qb_ref, k_ref, v_ref, gq_ref, gk_ref, gsum_ref, lam_ref, subg_ref,
                 o_ref, kn_scr, vt_scr, bias_scr, cfar_scr, qm_scr, m_scr, acc_scr, s0_scr, s1_scr,
                 *, tq, nq, lam_init, starts):
    b = pl.program_id(0)
    t = pl.program_id(1)
    tk = tq
    nw = N_MAPS * tq
    seq = k_ref.shape[0]
    DIAG, SUB, FAR = 0, 1, 2

    @pl.when((b == 0) & (t == 0))
    def _():
        qm_scr[...] = jnp.zeros(qm_scr.shape, BF16)
        kk = lax.broadcasted_iota(jnp.int32, (tk, tq), 0)
        qq = lax.broadcasted_iota(jnp.int32, (tk, tq), 1)
        for h in range(N_HEADS):
            far = tbl_ref[h * REL_BUCKETS + REL_BUCKETS - 1]
            cfar_scr[:, 2 * h * tq:(2 * h + 2) * tq] = jnp.full((1, 2 * tq), far, F32)
            bias_scr[FAR, :, 2 * h * tq:(2 * h + 2) * tq] = jnp.full((tk, 2 * tq), far, F32)
            for off in (DIAG, SUB):
                d = qq - kk + off * tk
                val = jnp.full((tk, tq), tbl_ref[h * REL_BUCKETS], F32)
                for bk in range(1, REL_BUCKETS):
                    val = jnp.where(d >= starts[bk], tbl_ref[h * REL_BUCKETS + bk], val)
                if off == DIAG:
                    val = jnp.where(d >= 0, val, NEG_INF)
                for mp in range(2):
                    bias_scr[off, :, (2 * h + mp) * tq:(2 * h + mp + 1) * tq] = val

    @pl.when(t == 0)
    def _():
        for c0 in range(0, seq, tk):
            kf = k_ref[c0:c0 + tk, :].astype(F32)
            kn_scr[c0:c0 + tk, :] = (kf * _group_rms(kf, gsum_ref) * gk_ref[...]).astype(BF16)
            vt = v_ref[c0:c0 + tk, :].astype(F32).T.astype(BF16)
            for h in range(N_HEADS):
                vt_scr[h, 0:HEAD_V, c0:c0 + tk] = vt[h * HEAD_V:(h + 1) * HEAD_V, :]
                vt_scr[h, HEAD_V:V_ROWS, c0:c0 + tk] = jnp.ones((V_ROWS - HEAD_V, tk), BF16)

    for x, q_ref in enumerate((qa_ref, qb_ref)):
        qf = q_ref[...].astype(F32)
        qt = (qf * _group_rms(qf, gsum_ref) * gq_ref[...]).T
        qt = qt.astype(BF16)
        for g in range(N_MAPS):
            rows = slice(g * HEAD_QK, (g + 1) * HEAD_QK)
            qm_scr[x, rows, g * tq:(g + 1) * tq] = qt[rows, :]
    m_scr[...] = jnp.full(m_scr.shape, NEG_INF, F32)
    acc_scr[...] = jnp.zeros(acc_scr.shape, F32)

    def scores(n, x, j, bias_tile, constant_bias):
        sbuf = (s0_scr, s1_scr)[n % 2].at[jnp.minimum(t, 0)]
        c0 = pl.multiple_of(j * tk, tk)
        s = jnp.dot(kn_scr[pl.ds(c0, tk), :], qm_scr[x], preferred_element_type=F32)
        m_old = m_scr[x]
        if constant_bias:
            sbuf[...] = s
            c = cfar_scr[...]
            m_new = jnp.maximum(m_old, jnp.max(s, axis=0, keepdims=True) + c)
            shift = m_new - c
        else:
            s = s + bias_scr[bias_tile]
            sbuf[...] = s
            m_new = jnp.maximum(m_old, jnp.max(s, axis=0, keepdims=True))
            shift = m_new
        m_scr[x] = m_new
        return x, c0, sbuf, shift, jnp.exp2(m_old - m_new)

    def accumulate(x, c0, sbuf, shift, alpha):
        pb = jnp.exp2(sbuf[...] - shift).astype(BF16)
        for h in range(N_HEADS):
            w = slice(2 * h * tq, (2 * h + 2) * tq)
            pv = jnp.dot(vt_scr[h, :, pl.ds(c0, tk)], pb[:, w], preferred_element_type=F32)
            acc_scr[x, h] = alpha[:, w] * acc_scr[x, h] + pv

    qa = t
    qb = nq - 1 - t
    has_sub = t > 0
    n_far_a = jnp.maximum(t - 1, 0)
    blocks = [(0, qa, DIAG, False), (1, qb, DIAG, False), (1, qb - 1, SUB, False),
              (jnp.where(has_sub, 0, 1), jnp.where(has_sub, qa - 1, nq - 3), jnp.where(has_sub, SUB, FAR), False)]
    for i in range(nq - 3):
        to_a = i < n_far_a
        blocks.append((jnp.where(to_a, 0, 1), jnp.where(to_a, i, i - n_far_a), FAR, True))
    staged = scores(0, *blocks[0])
    for n in range(len(blocks)):
        following = scores(n + 1, *blocks[n + 1]) if n + 1 < len(blocks) else None
        accumulate(*staged)
        staged = following

    lp = lam_ref[...]
    lam = (jnp.exp(jnp.sum(lp[0:1] * lp[1:2], axis=-1, keepdims=True))
           - jnp.exp(jnp.sum(lp[2:3] * lp[3:4], axis=-1, keepdims=True)) + lam_init)
    for x in range(2):
        heads = []
        for h in range(N_HEADS):
            r = acc_scr[x, h, 0:HEAD_V, :] / acc_scr[x, h, HEAD_V:HEAD_V + 1, :]
            o_h = r[:, :tq] - lam * r[:, tq:]
            ms = jnp.mean(o_h * o_h, axis=0, keepdims=True)
            heads.append(o_h * lax.rsqrt(ms + EPS) * subg_ref[...] * (1.0 - lam_init))
        o_ref[x] = jnp.concatenate(heads, axis=0).T.astype(o_ref.dtype)


def _attention(l, qkv, tbl, gq, gk, gsum, lam_params, subg_col, batch, seq, tq, lam_init):
    t = qkv.shape[0]
    nq = seq // tq
    assert nq % 2 == 0 and nq >= 4
    nh = nq // 2
    starts = _bucket_starts(2 * tq)
    assert starts[REL_BUCKETS - 1] <= tq, "blocks two or more tiles below the diagonal must sit in the last bucket"
    whole = functools.partial(_layer_block, l)
    kern = functools.partial(_attn_kernel, tq=tq, nq=nq, lam_init=lam_init, starts=starts)
    return pl.pallas_call(
        kern,
        grid=(batch, nh),
        in_specs=[
            pl.BlockSpec(memory_space=pltpu.SMEM),
            pl.BlockSpec((tq, GROUP_W), lambda b, i: (b * nq + i, 0)),
            pl.BlockSpec((tq, GROUP_W), lambda b, i: (b * nq + nq - 1 - i, 0)),
            pl.BlockSpec((seq, GROUP_W), lambda b, i: (b, 1)),
            pl.BlockSpec((seq, GROUP_W), lambda b, i: (b, 2)),
            whole((1, GROUP_W)),
            whole((1, GROUP_W)),
            pl.BlockSpec((GROUP_W, GROUP_W), lambda b, i: (0, 0)),
            whole((4, HEAD_QK)),
            whole((HEAD_V, 1)),
        ],
        out_specs=pl.BlockSpec((2, tq, GROUP_W), lambda b, i: (0, b * nh + i, 0)),
        out_shape=jax.ShapeDtypeStruct((2, t // 2, GROUP_W), BF16),
        scratch_shapes=[
            pltpu.VMEM((seq, GROUP_W), BF16),
            pltpu.VMEM((N_HEADS, V_ROWS, seq), BF16),
            pltpu.VMEM((3, tq, N_MAPS * tq), F32),
            pltpu.VMEM((1, N_MAPS * tq), F32),
            pltpu.VMEM((2, GROUP_W, N_MAPS * tq), BF16),
            pltpu.VMEM((2, 1, N_MAPS * tq), F32),
            pltpu.VMEM((2, N_HEADS, V_ROWS, 2 * tq), F32),
            pltpu.VMEM((1, tq, N_MAPS * tq), F32),
            pltpu.VMEM((1, tq, N_MAPS * tq), F32),
        ],
        compiler_params=pltpu.CompilerParams(
            dimension_semantics=("arbitrary", "arbitrary"), vmem_limit_bytes=VMEM_LIMIT),
        name="diff_attention",
    )(tbl, qkv, qkv, qkv, qkv, gq, gk, gsum, lam_params, subg_col)


def _out_ffn_kernel(x_ref, mix_ref, att0_ref, att1_ref, wo_ref, g2_ref, wg_ref, wu_ref, wd_ref, o_ref, a_scr,
                    *, ff_chunk):
    mix_w = mix_ref.shape[1]
    att = jnp.concatenate([att0_ref[...], att1_ref[...]], axis=0)
    x1 = (x_ref[...]
          + jnp.dot(mix_ref[...], wo_ref[0:mix_w, :], preferred_element_type=F32)
          + jnp.dot(att, wo_ref[mix_w:, :], preferred_element_type=F32))
    ms = jnp.mean(x1 * x1, axis=-1, keepdims=True)
    h2 = (x1 * lax.rsqrt(ms + EPS) * g2_ref[...]).astype(BF16)
    for c0 in range(0, D_FF, ff_chunk):
        gate = jnp.dot(h2, wg_ref[:, c0:c0 + ff_chunk], preferred_element_type=F32)
        up = jnp.dot(h2, wu_ref[:, c0:c0 + ff_chunk], preferred_element_type=F32)
        a_scr[:, c0:c0 + ff_chunk] = (gate * jax.nn.sigmoid(gate) * up).astype(BF16)
    o_ref[...] = x1 + jnp.dot(a_scr[...], wd_ref[...], preferred_element_type=F32)


def _out_ffn(l, x2d, mix, att, wo, g2, wg, wu, wd, tm, seq, tq):
    t = x2d.shape[0]
    nq = seq // tq
    nh = nq // 2
    assert tm == 2 * tq
    tiles_per_seq = seq // tm
    whole = functools.partial(_layer_block, l)
    single = pl.Buffered(1)

    def att_block(which):
        def index(i):
            bb = i // tiles_per_seq
            u = 2 * (i % tiles_per_seq) + which
            lower = u < nh
            return (jnp.where(lower, 0, 1), bb * nh + jnp.where(lower, u, nq - 1 - u), 0)
        return pl.BlockSpec((None, tq, GROUP_W), index)

    kern = functools.partial(_out_ffn_kernel, ff_chunk=256)
    return pl.pallas_call(
        kern,
        grid=(t // tm,),
        in_specs=[
            pl.BlockSpec((tm, D_MODEL), lambda i: (i, 0)),
            pl.BlockSpec((tm, mix.shape[1]), lambda i: (i, 0)),
            att_block(0),
            att_block(1),
            whole((D_MODEL, D_MODEL), pipeline_mode=single),
            whole((1, D_MODEL)),
            whole((D_MODEL, D_FF), pipeline_mode=single),
            whole((D_MODEL, D_FF), pipeline_mode=single),
            whole((D_FF, D_MODEL), pipeline_mode=single),
        ],
        out_specs=pl.BlockSpec((tm, D_MODEL), lambda i: (i, 0)),
        out_shape=jax.ShapeDtypeStruct((t, D_MODEL), F32),
        scratch_shapes=[pltpu.VMEM((tm, D_FF), BF16)],
        compiler_params=pltpu.CompilerParams(
            dimension_semantics=("arbitrary",), vmem_limit_bytes=VMEM_LIMIT),
        name="out_ffn",
    )(x2d, mix, att, att, wo, g2, wg, wu, wd)


def _block_diag_pool(pool_w):
    out = jnp.zeros((pool_w.shape[0], GROUP_W, GROUP_W), pool_w.dtype)
    for g in range(len(POOL_WINDOWS)):
        out = out.at[:, g * POOL_GROUP:(g + 1) * POOL_GROUP, g * POOL_GROUP:(g + 1) * POOL_GROUP].set(pool_w[:, g])
    return out


def kernel(x, norm1_g, w_in, pool_w, pool_scale, sconv_w, conf_dw_w, conf_dw_b, conf_ln_g, conf_ln_b,
           q_norm_g, k_norm_g, lam_q1, lam_k1, lam_q2, lam_k2, subln_g, w_out, norm2_g, w_gate, w_up,
           w_down, rel_bias):
    batch, seq, d = x.shape
    assert d == D_MODEL and w_in.shape[-1] == D_IN
    depth = w_in.shape[0]
    t = batch * seq
    tq = 256
    tm = 2 * tq
    ts = 1024
    assert seq % tm == 0 and seq % ts == 0 and ts % HALO == 0

    gid = np.arange(GROUP_W) // HEAD_QK
    gsum = jnp.asarray((gid[:, None] == gid[None, :]).astype(np.float32), BF16)
    tbl = (rel_bias.astype(F32).T * LOG2E).reshape(-1)
    rows = lambda v: v.astype(F32)[:, None, :]
    rep8 = lambda v: jnp.broadcast_to(v.astype(F32)[..., None, :], v.shape[:-1] + (8, v.shape[-1]))
    in_mix_params = (rows(norm1_g), w_in.astype(BF16), _block_diag_pool(pool_w).astype(BF16), rows(pool_scale),
                     sconv_w.astype(F32), rep8(conf_dw_w), rep8(conf_dw_b), rep8(conf_ln_g), rep8(conf_ln_b))
    gq = rows(jnp.tile(q_norm_g.astype(F32), (1, N_MAPS)) * (HEAD_QK ** -0.5 * LOG2E))
    gk = rows(jnp.tile(k_norm_g.astype(F32), (1, N_MAPS)))
    lam_params = jnp.stack([lam_q1, lam_k1, lam_q2, lam_k2], axis=1).astype(F32)
    subg_col = subln_g.astype(F32)[:, :, None]
    ffn_params = (w_out.astype(BF16), rows(norm2_g), w_gate.astype(BF16), w_up.astype(BF16), w_down.astype(BF16))

    x2d = x.reshape(t, D_MODEL)
    for l in range(depth):
        lam_init = 0.8 - 0.6 * math.exp(-0.3 * l)
        qkv, mix = _in_mix(l, x2d, *in_mix_params, seq, ts)
        att = _attention(l, qkv, tbl, gq, gk, gsum, lam_params, subg_col, batch, seq, tq, lam_init)
        x2d = _out_ffn(l, x2d, mix, att, *ffn_params, tm, seq, tq)
    return x2d.reshape(batch, seq, D_MODEL)
```

```python
import functools
import math

import numpy as np
import jax
import jax.numpy as jnp
from jax import lax
from jax.experimental import pallas as pl
from jax.experimental.pallas import tpu as pltpu

F32 = jnp.float32
BF16 = jnp.bfloat16

D_MODEL = 1024
GROUP_W = 256
N_SLICES = 9
D_IN = N_SLICES * GROUP_W
N_MIX_SLICES = 6
POOL_WINDOWS = (2, 4, 8, 16)
POOL_GROUP = GROUP_W // len(POOL_WINDOWS)
SCONV_K = 3
CONF_K = 31
N_HEADS = 4
HEAD_V = GROUP_W // N_HEADS
HEAD_QK = HEAD_V // 2
V_ROWS = HEAD_V + 16
N_MAPS = 2 * N_HEADS
D_FF = 2816
REL_BUCKETS = 32
REL_MAX_EXACT = 16
REL_MAX_DIST = 128
EPS = 1e-6
NEG_INF = -1e30
LOG2E = math.log2(math.e)

HALO = 32
VMEM_LIMIT = 56 * 1024 * 1024


def _bucket_starts(max_dist):
    n = np.arange(0, max_dist + 1)
    nf = np.maximum(n, 1).astype(np.float32)
    large = REL_MAX_EXACT + (np.log(nf / np.float32(REL_MAX_EXACT)) / np.float32(math.log(REL_MAX_DIST / REL_MAX_EXACT))
                             * np.float32(REL_BUCKETS - REL_MAX_EXACT)).astype(np.int32)
    large = np.minimum(large, REL_BUCKETS - 1)
    bucket = np.where(n < REL_MAX_EXACT, n, large)
    starts = [int(np.argmax(bucket >= b)) for b in range(REL_BUCKETS)]
    assert bucket[-1] == REL_BUCKETS - 1
    return starts


def _layer_block(l, shape, **kwargs):
    index = (l,) + (0,) * len(shape)
    return pl.BlockSpec((None,) + tuple(shape), lambda *g: index, **kwargs)


def _in_mix_kernel(x_ref, g1_ref, w_ref, poolw_ref, pscale_ref, sconvw_ref, cw_ref, cb_ref, lng_ref, lnb_ref,
                   qkv_ref, mix_ref, zbuf, a0, b1, b2, b3, p_ext, u_ext, u_sh, h_scr, *, ts, nst, row_chunk):
    n = HALO + ts
    W = GROUP_W
    g = pl.program_id(0)
    sm = lax.rem(g + nst - 1, nst)

    @pl.when(g == 0)
    def _():
        zbuf[...] = jnp.zeros(zbuf.shape, F32)

    @pl.when(sm == 0)
    def _():
        a0[0:HALO, :] = jnp.zeros((HALO, W), F32)
        p_ext[0:8, :] = jnp.zeros((8, W), F32)
        u_ext[0:HALO, :] = jnp.zeros((HALO, W), F32)

    a0[HALO:n, :] = zbuf[:, 0:W]
    p_ext[8:8 + ts, :] = zbuf[:, 2 * W:3 * W] * zbuf[:, 3 * W:4 * W]
    conv3 = sconvw_ref[SCONV_K - 1:SCONV_K, :] * p_ext[8:8 + ts, :]
    for j in range(1, SCONV_K):
        conv3 = conv3 + sconvw_ref[SCONV_K - 1 - j:SCONV_K - j, :] * p_ext[8 - j:8 - j + ts, :]
    mix_ref[:, W:2 * W] = (zbuf[:, W:2 * W] * conv3).astype(mix_ref.dtype)
    p_ext[0:8, :] = p_ext[ts:ts + 8, :]
    u_ext[HALO:n, :] = zbuf[:, 4 * W:5 * W] * jax.nn.sigmoid(zbuf[:, 5 * W:6 * W])
    for r in range(1, 8):
        u_sh[r, 8:n, :] = u_ext[8 - r:n - r, :]

    x = x_ref[...]
    ms = jnp.mean(x * x, axis=-1, keepdims=True)
    h = h_scr.at[jnp.minimum(g, 0)]
    h[...] = (x_ref[...] * lax.rsqrt(ms + EPS) * g1_ref[...]).astype(BF16)

    def project(k):
        zk = jnp.dot(h[...], w_ref[:, k * W:(k + 1) * W], preferred_element_type=F32)
        if k < N_MIX_SLICES:
            zbuf[:, k * W:(k + 1) * W] = zk
        else:
            qkv_ref[:, (k - N_MIX_SLICES) * W:(k - N_MIX_SLICES + 1) * W] = zk.astype(qkv_ref.dtype)

    def rows8(v):
        return v.reshape(v.shape[0] // 8, 8, W)

    chunks = list(range(0, ts, row_chunk))
    assert len(chunks) >= N_SLICES - 1
    for ci, r0 in enumerate(chunks):
        if ci < N_SLICES - 1:
            project(ci)
        acc = cb_ref[...] + cw_ref[CONF_K - 1] * rows8(u_ext[HALO + r0:HALO + r0 + row_chunk, :])
        for j in range(1, CONF_K):
            q8, r = 8 * (j // 8), j % 8
            lo = HALO + r0 - q8
            src = u_ext[lo:lo + row_chunk, :] if r == 0 else u_sh[r, lo:lo + row_chunk, :]
            acc = acc + cw_ref[CONF_K - 1 - j] * rows8(src)
        mu = jnp.mean(acc, axis=-1, keepdims=True)
        xc = acc - mu
        var = jnp.mean(xc * xc, axis=-1, keepdims=True)
        yn = xc * lax.rsqrt(var + EPS) * lng_ref[...] + lnb_ref[...]
        mix_ref[r0:r0 + row_chunk, 2 * W:3 * W] = (
            (yn * jax.nn.sigmoid(yn)).reshape(row_chunk, W).astype(mix_ref.dtype))
    u_ext[0:HALO, :] = u_ext[ts:n, :]
    project(N_SLICES - 1)

    b1[8:n, :] = a0[8:n, :] + a0[7:n - 1, :]
    b2[16:n, :] = b1[16:n, :] + b1[14:n - 2, :]
    b3[24:n, :] = b2[24:n, :] + b2[20:n - 4, :]
    s16 = b3[HALO:n, :] + b3[HALO - 8:n - 8, :]
    lane = lax.broadcasted_iota(jnp.int32, (ts, W), 1)
    pos1 = sm * ts + lax.broadcasted_iota(jnp.int32, (ts, W), 0) + 1
    sel = jnp.where(lane < POOL_GROUP, b1[HALO:n, :],
                    jnp.where(lane < 2 * POOL_GROUP, b2[HALO:n, :],
                              jnp.where(lane < 3 * POOL_GROUP, b3[HALO:n, :], s16)))
    win = jnp.where(lane < POOL_GROUP, POOL_WINDOWS[0],
                    jnp.where(lane < 2 * POOL_GROUP, POOL_WINDOWS[1],
                              jnp.where(lane < 3 * POOL_GROUP, POOL_WINDOWS[2], POOL_WINDOWS[3])))
    cnt = jnp.minimum(pos1, win).astype(F32)
    pooled = sel / cnt - a0[HALO:n, :]
    y_pool = jnp.dot(pooled.astype(BF16), poolw_ref[...], preferred_element_type=F32) * pscale_ref[...]
    mix_ref[:, 0:W] = y_pool.astype(mix_ref.dtype)
    a0[0:HALO, :] = a0[ts:n, :]


def _in_mix(l, x2d, g1, w, poolw_bd, pscale, sconvw, cw, cb, lng, lnb, seq, ts):
    t = x2d.shape[0]
    nst = seq // ts
    n_tiles = t // ts
    whole = functools.partial(_layer_block, l)
    kern = functools.partial(_in_mix_kernel, ts=ts, nst=nst, row_chunk=32)
    out_w = 3 * GROUP_W
    return pl.pallas_call(
        kern,
        grid=(n_tiles + 1,),
        in_specs=[
            pl.BlockSpec((ts, D_MODEL), lambda g: (jnp.minimum(g, n_tiles - 1), 0)),
            whole((1, D_MODEL)),
            whole((D_MODEL, D_IN), pipeline_mode=pl.Buffered(1)),
            whole((GROUP_W, GROUP_W)),
            whole((1, GROUP_W)),
            whole((SCONV_K, GROUP_W)),
            whole((CONF_K, 8, GROUP_W)),
            whole((8, GROUP_W)),
            whole((8, GROUP_W)),
            whole((8, GROUP_W)),
        ],
        out_specs=[pl.BlockSpec((ts, out_w), lambda g: (jnp.minimum(g, n_tiles - 1), 0)),
                   pl.BlockSpec((ts, out_w), lambda g: (jnp.maximum(g - 1, 0), 0))],
        out_shape=[jax.ShapeDtypeStruct((t, out_w), BF16), jax.ShapeDtypeStruct((t, out_w), BF16)],
        scratch_shapes=[
            pltpu.VMEM((ts, N_MIX_SLICES * GROUP_W), F32),
            pltpu.VMEM((HALO + ts, GROUP_W), F32),
            pltpu.VMEM((HALO + ts, GROUP_W), F32),
            pltpu.VMEM((HALO + ts, GROUP_W), F32),
            pltpu.VMEM((HALO + ts, GROUP_W), F32),
            pltpu.VMEM((8 + ts, GROUP_W), F32),
            pltpu.VMEM((HALO + ts, GROUP_W), F32),
            pltpu.VMEM((8, HALO + ts, GROUP_W), F32),
            pltpu.VMEM((1, ts, D_MODEL), BF16),
        ],
        compiler_params=pltpu.CompilerParams(
            dimension_semantics=("arbitrary",), vmem_limit_bytes=VMEM_LIMIT),
        name="in_mix",
    )(x2d, g1, w, poolw_bd, pscale, sconvw, cw, cb, lng, lnb)


def _group_rms(x, gsum_ref):
    x2 = x * x
    hi = x2.astype(BF16)
    lo = (x2 - hi.astype(F32)).astype(BF16)
    ssq = (jnp.dot(hi, gsum_ref[...], preferred_element_type=F32)
           + jnp.dot(lo, gsum_ref[...], preferred_element_type=F32))
    return lax.rsqrt(ssq * (1.0 / HEAD_QK) + EPS)


def _attn_kernel(tbl_ref, qa_ref, qb_ref, k_ref, v_ref, gq_ref, gk_ref, gsum_ref, lam_ref, subg_ref,
                 o_ref, kn_scr, vt_scr, bias_scr, cfar_scr, qm_scr, m_scr, acc_scr, s0_scr, s1_scr,
                 *, tq, nq, lam_init, starts):
    b = pl.program_id(0)
    t = pl.program_id(1)
    tk = tq
    nw = N_MAPS * tq
    seq = k_ref.shape[0]
    DIAG, SUB, FAR = 0, 1, 2

    @pl.when((b == 0) & (t == 0))
    def _():
        qm_scr[...] = jnp.zeros(qm_scr.shape, BF16)
        kk = lax.broadcasted_iota(jnp.int32, (tk, tq), 0)
        qq = lax.broadcasted_iota(jnp.int32, (tk, tq), 1)
        for h in range(N_HEADS):
            far = tbl_ref[h * REL_BUCKETS + REL_BUCKETS - 1]
            cfar_scr[:, 2 * h * tq:(2 * h + 2) * tq] = jnp.full((1, 2 * tq), far, F32)
            bias_scr[FAR, :, 2 * h * tq:(2 * h + 2) * tq] = jnp.full((tk, 2 * tq), far, F32)
            for off in (DIAG, SUB):
                d = qq - kk + off * tk
                val = jnp.full((tk, tq), tbl_ref[h * REL_BUCKETS], F32)
                for bk in range(1, REL_BUCKETS):
                    val = jnp.where(d >= starts[bk], tbl_ref[h * REL_BUCKETS + bk], val)
                if off == DIAG:
                    val = jnp.where(d >= 0, val, NEG_INF)
                for mp in range(2):
                    bias_scr[off, :, (2 * h + mp) * tq:(2 * h + mp + 1) * tq] = val

    @pl.when(t == 0)
    def _():
        for c0 in range(0, seq, tk):
            kf = k_ref[c0:c0 + tk, :].astype(F32)
            kn_scr[c0:c0 + tk, :] = (kf * _group_rms(kf, gsum_ref) * gk_ref[...]).astype(BF16)
            vt = v_ref[c0:c0 + tk, :].astype(F32).T.astype(BF16)
            for h in range(N_HEADS):
                vt_scr[h, 0:HEAD_V, c0:c0 + tk] = vt[h * HEAD_V:(h + 1) * HEAD_V, :]
                vt_scr[h, HEAD_V:V_ROWS, c0:c0 + tk] = jnp.ones((V_ROWS - HEAD_V, tk), BF16)

    for x, q_ref in enumerate((qa_ref, qb_ref)):
        qf = q_ref[...].astype(F32)
        qt = (qf * _group_rms(qf, gsum_ref) * gq_ref[...]).T
        qt = qt.astype(BF16)
        for g in range(N_MAPS):
            rows = slice(g * HEAD_QK, (g + 1) * HEAD_QK)
            qm_scr[x, rows, g * tq:(g + 1) * tq] = qt[rows, :]
    m_scr[...] = jnp.full(m_scr.shape, NEG_INF, F32)
    acc_scr[...] = jnp.zeros(acc_scr.shape, F32)

    def scores(n, x, j, bias_tile, constant_bias):
        sbuf = (s0_scr, s1_scr)[n % 2].at[jnp.minimum(t, 0)]
        c0 = pl.multiple_of(j * tk, tk)
        s = jnp.dot(kn_scr[pl.ds(c0, tk), :], qm_scr[x], preferred_element_type=F32)
        m_old = m_scr[x]
        if constant_bias:
            sbuf[...] = s
            c = cfar_scr[...]
            m_new = jnp.maximum(m_old, jnp.max(s, axis=0, keepdims=True) + c)
            shift = m_new - c
        else:
            s = s + bias_scr[bias_tile]
            sbuf[...] = s
            m_new = jnp.maximum(m_old, jnp.max(s, axis=0, keepdims=True))
            shift = m_new
        m_scr[x] = m_new
        return x, c0, sbuf, shift, jnp.exp2(m_old - m_new)

    def accumulate(x, c0, sbuf, shift, alpha):
        pb = jnp.exp2(sbuf[...] - shift).astype(BF16)
        for h in range(N_HEADS):
            w = slice(2 * h * tq, (2 * h + 2) * tq)
            pv = jnp.dot(vt_scr[h, :, pl.ds(c0, tk)], pb[:, w], preferred_element_type=F32)
            acc_scr[x, h] = alpha[:, w] * acc_scr[x, h] + pv

    qa = t
    qb = nq - 1 - t
    has_sub = t > 0
    n_far_a = jnp.maximum(t - 1, 0)
    blocks = [(0, qa, DIAG, False), (1, qb, DIAG, False), (1, qb - 1, SUB, False),
              (jnp.where(has_sub, 0, 1), jnp.where(has_sub, qa - 1, nq - 3), jnp.where(has_sub, SUB, FAR), False)]
    for i in range(nq - 3):
        to_a = i < n_far_a
        blocks.append((jnp.where(to_a, 0, 1), jnp.where(to_a, i, i - n_far_a), FAR, True))
    staged = scores(0, *blocks[0])
    for n in range(len(blocks)):
        following = scores(n + 1, *blocks[n + 1]) if n + 1 < len(blocks) else None
        accumulate(*staged)
        staged = following

    lp = lam_ref[...]
    lam = (jnp.exp(jnp.sum(lp[0:1] * lp[1:2], axis=-1, keepdims=True))
           - jnp.exp(jnp.sum(lp[2:3] * lp[3:4], axis=-1, keepdims=True)) + lam_init)
    for x in range(2):
        heads = []
        for h in range(N_HEADS):
            r = acc_scr[x, h, 0:HEAD_V, :] / acc_scr[x, h, HEAD_V:HEAD_V + 1, :]
            o_h = r[:, :tq] - lam * r[:, tq:]
            ms = jnp.mean(o_h * o_h, axis=0, keepdims=True)
            heads.append(o_h * lax.rsqrt(ms + EPS) * subg_ref[...] * (1.0 - lam_init))
        o_ref[x] = jnp.concatenate(heads, axis=0).T.astype(o_ref.dtype)


def _attention(l, qkv, tbl, gq, gk, gsum, lam_params, subg_col, batch, seq, tq, lam_init):
    t = qkv.shape[0]
    nq = seq // tq
    assert nq % 2 == 0 and nq >= 4
    nh = nq // 2
    starts = _bucket_starts(2 * tq)
    assert starts[REL_BUCKETS - 1] <= tq, "blocks two or more tiles below the diagonal must sit in the last bucket"
    whole = functools.partial(_layer_block, l)
    kern = functools.partial(_attn_kernel, tq=tq, nq=nq, lam_init=lam_init, starts=starts)
    return pl.pallas_call(
        kern,
        grid=(batch, nh),
        in_specs=[
            pl.BlockSpec(memory_space=pltpu.SMEM),
            pl.BlockSpec((tq, GROUP_W), lambda b, i: (b * nq + i, 0)),
            pl.BlockSpec((tq, GROUP_W), lambda b, i: (b * nq + nq - 1 - i, 0)),
            pl.BlockSpec((seq, GROUP_W), lambda b, i: (b, 1)),
            pl.BlockSpec((seq, GROUP_W), lambda b, i: (b, 2)),
            whole((1, GROUP_W)),
            whole((1, GROUP_W)),
            pl.BlockSpec((GROUP_W, GROUP_W), lambda b, i: (0, 0)),
            whole((4, HEAD_QK)),
            whole((HEAD_V, 1)),
        ],
        out_specs=pl.BlockSpec((2, tq, GROUP_W), lambda b, i: (0, b * nh + i, 0)),
        out_shape=jax.ShapeDtypeStruct((2, t // 2, GROUP_W), BF16),
        scratch_shapes=[
            pltpu.VMEM((seq, GROUP_W), BF16),
            pltpu.VMEM((N_HEADS, V_ROWS, seq), BF16),
            pltpu.VMEM((3, tq, N_MAPS * tq), F32),
            pltpu.VMEM((1, N_MAPS * tq), F32),
            pltpu.VMEM((2, GROUP_W, N_MAPS * tq), BF16),
            pltpu.VMEM((2, 1, N_MAPS * tq), F32),
            pltpu.VMEM((2, N_HEADS, V_ROWS, 2 * tq), F32),
            pltpu.VMEM((1, tq, N_MAPS * tq), F32),
            pltpu.VMEM((1, tq, N_MAPS * tq), F32),
        ],
        compiler_params=pltpu.CompilerParams(
            dimension_semantics=("arbitrary", "arbitrary"), vmem_limit_bytes=VMEM_LIMIT),
        name="diff_attention",
    )(tbl, qkv, qkv, qkv, qkv, gq, gk, gsum, lam_params, subg_col)


def _out_ffn_kernel(x_ref, mix_ref, *rest, n_att, ff_chunk):
    att_refs = rest[:n_att]
    wo_ref, g2_ref, wg_ref, wu_ref, wd_ref, o_ref, h2_scr, a_scr = rest[n_att:]
    mix_w = mix_ref.shape[1]
    att = jnp.concatenate([r[...] for r in att_refs], axis=0)
    o_ref[...] = (x_ref[...]
                  + jnp.dot(mix_ref[...], wo_ref[0:mix_w, :], preferred_element_type=F32)
                  + jnp.dot(att, wo_ref[mix_w:, :], preferred_element_type=F32))
    x1 = o_ref[...]
    ms = jnp.mean(x1 * x1, axis=-1, keepdims=True)
    h2_scr[...] = (o_ref[...] * lax.rsqrt(ms + EPS) * g2_ref[...]).astype(BF16)
    for c0 in range(0, D_FF, ff_chunk):
        gate = jnp.dot(h2_scr[...], wg_ref[:, c0:c0 + ff_chunk], preferred_element_type=F32)
        up = jnp.dot(h2_scr[...], wu_ref[:, c0:c0 + ff_chunk], preferred_element_type=F32)
        a_scr[:, c0:c0 + ff_chunk] = (gate * jax.nn.sigmoid(gate) * up).astype(BF16)
    o_ref[...] += jnp.dot(a_scr[...], wd_ref[...], preferred_element_type=F32)


def _out_ffn(l, x2d, mix, att, wo, g2, wg, wu, wd, tm, seq, tq):
    t = x2d.shape[0]
    nq = seq // tq
    nh = nq // 2
    n_att = tm // tq
    assert tm == n_att * tq and seq % tm == 0
    tiles_per_seq = seq // tm
    whole = functools.partial(_layer_block, l)
    single = pl.Buffered(1)

    def att_block(which):
        def index(i):
            bb = i // tiles_per_seq
            u = n_att * (i % tiles_per_seq) + which
            lower = u < nh
            return (jnp.where(lower, 0, 1), bb * nh + jnp.where(lower, u, nq - 1 - u), 0)
        return pl.BlockSpec((None, tq, GROUP_W), index)

    kern = functools.partial(_out_ffn_kernel, n_att=n_att, ff_chunk=256)
    return pl.pallas_call(
        kern,
        grid=(t // tm,),
        in_specs=[
            pl.BlockSpec((tm, D_MODEL), lambda i: (i, 0)),
            pl.BlockSpec((tm, mix.shape[1]), lambda i: (i, 0)),
            *[att_block(which) for which in range(n_att)],
            whole((D_MODEL, D_MODEL), pipeline_mode=single),
            whole((1, D_MODEL)),
            whole((D_MODEL, D_FF), pipeline_mode=single),
            whole((D_MODEL, D_FF), pipeline_mode=single),
            whole((D_FF, D_MODEL), pipeline_mode=single),
        ],
        out_specs=pl.BlockSpec((tm, D_MODEL), lambda i: (i, 0)),
        out_shape=jax.ShapeDtypeStruct((t, D_MODEL), F32),
        scratch_shapes=[pltpu.VMEM((tm, D_MODEL), BF16),
                        pltpu.VMEM((tm, D_FF), BF16)],
        compiler_params=pltpu.CompilerParams(
            dimension_semantics=("arbitrary",), vmem_limit_bytes=VMEM_LIMIT),
        name="out_ffn",
    )(x2d, mix, *([att] * n_att), wo, g2, wg, wu, wd)


def _block_diag_pool(pool_w):
    out = jnp.zeros((pool_w.shape[0], GROUP_W, GROUP_W), pool_w.dtype)
    for g in range(len(POOL_WINDOWS)):
        out = out.at[:, g * POOL_GROUP:(g + 1) * POOL_GROUP, g * POOL_GROUP:(g + 1) * POOL_GROUP].set(pool_w[:, g])
    return out


def kernel(x, norm1_g, w_in, pool_w, pool_scale, sconv_w, conf_dw_w, conf_dw_b, conf_ln_g, conf_ln_b,
           q_norm_g, k_norm_g, lam_q1, lam_k1, lam_q2, lam_k2, subln_g, w_out, norm2_g, w_gate, w_up,
           w_down, rel_bias):
    batch, seq, d = x.shape
    assert d == D_MODEL and w_in.shape[-1] == D_IN
    depth = w_in.shape[0]
    t = batch * seq
    tq = 256
    tm = 4 * tq
    ts = 1024
    assert seq % tm == 0 and seq % ts == 0 and ts % HALO == 0

    gid = np.arange(GROUP_W) // HEAD_QK
    gsum = jnp.asarray((gid[:, None] == gid[None, :]).astype(np.float32), BF16)
    tbl = (rel_bias.astype(F32).T * LOG2E).reshape(-1)
    rows = lambda v: v.astype(F32)[:, None, :]
    rep8 = lambda v: jnp.broadcast_to(v.astype(F32)[..., None, :], v.shape[:-1] + (8, v.shape[-1]))
    in_mix_params = (rows(norm1_g), w_in.astype(BF16), _block_diag_pool(pool_w).astype(BF16), rows(pool_scale),
                     sconv_w.astype(F32), rep8(conf_dw_w), rep8(conf_dw_b), rep8(conf_ln_g), rep8(conf_ln_b))
    gq = rows(jnp.tile(q_norm_g.astype(F32), (1, N_MAPS)) * (HEAD_QK ** -0.5 * LOG2E))
    gk = rows(jnp.tile(k_norm_g.astype(F32), (1, N_MAPS)))
    lam_params = jnp.stack([lam_q1, lam_k1, lam_q2, lam_k2], axis=1).astype(F32)
    subg_col = subln_g.astype(F32)[:, :, None]
    ffn_params = (w_out.astype(BF16), rows(norm2_g), w_gate.astype(BF16), w_up.astype(BF16), w_down.astype(BF16))

    x2d = x.reshape(t, D_MODEL)
    for l in range(depth):
        lam_init = 0.8 - 0.6 * math.exp(-0.3 * l)
        qkv, mix = _in_mix(l, x2d, *in_mix_params, seq, ts)
        att = _attention(l, qkv, tbl, gq, gk, gsum, lam_params, subg_col, batch, seq, tq, lam_init)
        x2d = _out_ffn(l, x2d, mix, att, *ffn_params, tm, seq, tq)
    return x2d.reshape(batch, seq, D_MODEL)
```

```python
import functools
import math

import numpy as np
import jax
import jax.numpy as jnp
from jax import lax
from jax.experimental import pallas as pl
from jax.experimental.pallas import tpu as pltpu

F32 = jnp.float32
BF16 = jnp.bfloat16

D_MODEL = 1024
GROUP_W = 256
N_SLICES = 9
D_IN = N_SLICES * GROUP_W
N_MIX_SLICES = 6
POOL_WINDOWS = (2, 4, 8, 16)
POOL_GROUP = GROUP_W // len(POOL_WINDOWS)
SCONV_K = 3
CONF_K = 31
N_HEADS = 4
HEAD_V = GROUP_W // N_HEADS
HEAD_QK = HEAD_V // 2
V_ROWS = HEAD_V + 16
N_MAPS = 2 * N_HEADS
D_FF = 2816
REL_BUCKETS = 32
REL_MAX_EXACT = 16
REL_MAX_DIST = 128
EPS = 1e-6
NEG_INF = -1e30
LOG2E = math.log2(math.e)

HALO = 32
VMEM_LIMIT = 56 * 1024 * 1024


def _bucket_starts(max_dist):
    n = np.arange(0, max_dist + 1)
    nf = np.maximum(n, 1).astype(np.float32)
    large = REL_MAX_EXACT + (np.log(nf / np.float32(REL_MAX_EXACT)) / np.float32(math.log(REL_MAX_DIST / REL_MAX_EXACT))
                             * np.float32(REL_BUCKETS - REL_MAX_EXACT)).astype(np.int32)
    large = np.minimum(large, REL_BUCKETS - 1)
    bucket = np.where(n < REL_MAX_EXACT, n, large)
    starts = [int(np.argmax(bucket >= b)) for b in range(REL_BUCKETS)]
    assert bucket[-1] == REL_BUCKETS - 1
    return starts


def _layer_block(l, shape, **kwargs):
    index = (l,) + (0,) * len(shape)
    return pl.BlockSpec((None,) + tuple(shape), lambda *g: index, **kwargs)


def _in_mix_kernel(x_ref, g1_ref, w_ref, poolw_ref, pscale_ref, sconvw_ref, cw_ref, cb_ref, lng_ref, lnb_ref,
                   qkv_ref, mix_ref, zbuf, a0, b1, b2, b3, p_ext, u_ext, u_sh, h_scr, *, ts, nst, row_chunk):
    n = HALO + ts
    W = GROUP_W
    g = pl.program_id(0)
    sm = lax.rem(g + nst - 1, nst)

    @pl.when(g == 0)
    def _():
        zbuf[...] = jnp.zeros(zbuf.shape, F32)

    @pl.when(sm == 0)
    def _():
        a0[0:HALO, :] = jnp.zeros((HALO, W), F32)
        p_ext[0:8, :] = jnp.zeros((8, W), F32)
        u_ext[0:HALO, :] = jnp.zeros((HALO, W), F32)

    a0[HALO:n, :] = zbuf[:, 0:W]
    p_ext[8:8 + ts, :] = zbuf[:, 2 * W:3 * W] * zbuf[:, 3 * W:4 * W]
    conv3 = sconvw_ref[SCONV_K - 1:SCONV_K, :] * p_ext[8:8 + ts, :]
    for j in range(1, SCONV_K):
        conv3 = conv3 + sconvw_ref[SCONV_K - 1 - j:SCONV_K - j, :] * p_ext[8 - j:8 - j + ts, :]
    mix_ref[:, W:2 * W] = (zbuf[:, W:2 * W] * conv3).astype(mix_ref.dtype)
    p_ext[0:8, :] = p_ext[ts:ts + 8, :]
    u_ext[HALO:n, :] = zbuf[:, 4 * W:5 * W] * jax.nn.sigmoid(zbuf[:, 5 * W:6 * W])
    for r in range(1, 8):
        u_sh[r, 8:n, :] = u_ext[8 - r:n - r, :]

    x = x_ref[...]
    ms = jnp.mean(x * x, axis=-1, keepdims=True)
    h = h_scr.at[jnp.minimum(g, 0)]
    h[...] = (x_ref[...] * lax.rsqrt(ms + EPS) * g1_ref[...]).astype(BF16)

    def project(k):
        zk = jnp.dot(h[...], w_ref[:, k * W:(k + 1) * W], preferred_element_type=F32)
        if k < N_MIX_SLICES:
            zbuf[:, k * W:(k + 1) * W] = zk
        else:
            qkv_ref[:, (k - N_MIX_SLICES) * W:(k - N_MIX_SLICES + 1) * W] = zk.astype(qkv_ref.dtype)

    def rows8(v):
        return v.reshape(v.shape[0] // 8, 8, W)

    chunks = list(range(0, ts, row_chunk))
    assert len(chunks) >= N_SLICES - 1
    for ci, r0 in enumerate(chunks):
        if ci < N_SLICES - 1:
            project(ci)
        acc = cb_ref[...] + cw_ref[CONF_K - 1] * rows8(u_ext[HALO + r0:HALO + r0 + row_chunk, :])
        for j in range(1, CONF_K):
            q8, r = 8 * (j // 8), j % 8
            lo = HALO + r0 - q8
            src = u_ext[lo:lo + row_chunk, :] if r == 0 else u_sh[r, lo:lo + row_chunk, :]
            acc = acc + cw_ref[CONF_K - 1 - j] * rows8(src)
        mu = jnp.mean(acc, axis=-1, keepdims=True)
        xc = acc - mu
        var = jnp.mean(xc * xc, axis=-1, keepdims=True)
        yn = xc * lax.rsqrt(var + EPS) * lng_ref[...] + lnb_ref[...]
        mix_ref[r0:r0 + row_chunk, 2 * W:3 * W] = (
            (yn * jax.nn.sigmoid(yn)).reshape(row_chunk, W).astype(mix_ref.dtype))
    u_ext[0:HALO, :] = u_ext[ts:n, :]
    project(N_SLICES - 1)

    b1[8:n, :] = a0[8:n, :] + a0[7:n - 1, :]
    b2[16:n, :] = b1[16:n, :] + b1[14:n - 2, :]
    b3[24:n, :] = b2[24:n, :] + b2[20:n - 4, :]
    s16 = b3[HALO:n, :] + b3[HALO - 8:n - 8, :]
    lane = lax.broadcasted_iota(jnp.int32, (ts, W), 1)
    pos1 = sm * ts + lax.broadcasted_iota(jnp.int32, (ts, W), 0) + 1
    sel = jnp.where(lane < POOL_GROUP, b1[HALO:n, :],
                    jnp.where(lane < 2 * POOL_GROUP, b2[HALO:n, :],
                              jnp.where(lane < 3 * POOL_GROUP, b3[HALO:n, :], s16)))
    win = jnp.where(lane < POOL_GROUP, POOL_WINDOWS[0],
                    jnp.where(lane < 2 * POOL_GROUP, POOL_WINDOWS[1],
                              jnp.where(lane < 3 * POOL_GROUP, POOL_WINDOWS[2], POOL_WINDOWS[3])))
    cnt = jnp.minimum(pos1, win).astype(F32)
    pooled = sel / cnt - a0[HALO:n, :]
    y_pool = jnp.dot(pooled.astype(BF16), poolw_ref[...], preferred_element_type=F32) * pscale_ref[...]
    mix_ref[:, 0:W] = y_pool.astype(mix_ref.dtype)
    a0[0:HALO, :] = a0[ts:n, :]


def _in_mix(l, x2d, g1, w, poolw_bd, pscale, sconvw, cw, cb, lng, lnb, seq, ts):
    t = x2d.shape[0]
    nst = seq // ts
    n_tiles = t // ts
    whole = functools.partial(_layer_block, l)
    kern = functools.partial(_in_mix_kernel, ts=ts, nst=nst, row_chunk=32)
    out_w = 3 * GROUP_W
    return pl.pallas_call(
        kern,
        grid=(n_tiles + 1,),
        in_specs=[
            pl.BlockSpec((ts, D_MODEL), lambda g: (jnp.minimum(g, n_tiles - 1), 0)),
            whole((1, D_MODEL)),
            whole((D_MODEL, D_IN), pipeline_mode=pl.Buffered(1)),
            whole((GROUP_W, GROUP_W)),
            whole((1, GROUP_W)),
            whole((SCONV_K, GROUP_W)),
            whole((CONF_K, 8, GROUP_W)),
            whole((8, GROUP_W)),
            whole((8, GROUP_W)),
            whole((8, GROUP_W)),
        ],
        out_specs=[pl.BlockSpec((ts, out_w), lambda g: (jnp.minimum(g, n_tiles - 1), 0)),
                   pl.BlockSpec((ts, out_w), lambda g: (jnp.maximum(g - 1, 0), 0))],
        out_shape=[jax.ShapeDtypeStruct((t, out_w), BF16), jax.ShapeDtypeStruct((t, out_w), BF16)],
        scratch_shapes=[
            pltpu.VMEM((ts, N_MIX_SLICES * GROUP_W), F32),
            pltpu.VMEM((HALO + ts, GROUP_W), F32),
            pltpu.VMEM((HALO + ts, GROUP_W), F32),
            pltpu.VMEM((HALO + ts, GROUP_W), F32),
            pltpu.VMEM((HALO + ts, GROUP_W), F32),
            pltpu.VMEM((8 + ts, GROUP_W), F32),
            pltpu.VMEM((HALO + ts, GROUP_W), F32),
            pltpu.VMEM((8, HALO + ts, GROUP_W), F32),
            pltpu.VMEM((1, ts, D_MODEL), BF16),
        ],
        compiler_params=pltpu.CompilerParams(
            dimension_semantics=("arbitrary",), vmem_limit_bytes=VMEM_LIMIT),
        name="in_mix",
    )(x2d, g1, w, poolw_bd, pscale, sconvw, cw, cb, lng, lnb)


def _group_rms(x, gsum_ref):
    x2 = x * x
    hi = x2.astype(BF16)
    lo = (x2 - hi.astype(F32)).astype(BF16)
    ssq = (jnp.dot(hi, gsum_ref[...], preferred_element_type=F32)
           + jnp.dot(lo, gsum_ref[...], preferred_element_type=F32))
    return lax.rsqrt(ssq * (1.0 / HEAD_QK) + EPS)


def _attn_kernel(tbl_ref, qa_ref, qb_ref, k_ref, v_ref, gq_ref, gk_ref, gsum_ref, lam_ref, subg_ref,
                 o_ref, kn_scr, vt_scr, bias_scr, cfar_scr, qm_scr, m_scr, acc_scr, s0_scr, s1_scr,
                 *, tq, nq, pairs, lam_init, starts):
    b = pl.program_id(0)
    u = pl.program_id(1)
    tk = tq
    nw = N_MAPS * tq
    seq = k_ref.shape[0]
    DIAG, SUB, FAR = 0, 1, 2

    @pl.when((b == 0) & (u == 0))
    def _():
        qm_scr[...] = jnp.zeros(qm_scr.shape, BF16)
        kk = lax.broadcasted_iota(jnp.int32, (tk, tq), 0)
        qq = lax.broadcasted_iota(jnp.int32, (tk, tq), 1)
        for h in range(N_HEADS):
            far = tbl_ref[h * REL_BUCKETS + REL_BUCKETS - 1]
            cfar_scr[:, 2 * h * tq:(2 * h + 2) * tq] = jnp.full((1, 2 * tq), far, F32)
            bias_scr[FAR, :, 2 * h * tq:(2 * h + 2) * tq] = jnp.full((tk, 2 * tq), far, F32)
            for off in (DIAG, SUB):
                d = qq - kk + off * tk
                val = jnp.full((tk, tq), tbl_ref[h * REL_BUCKETS], F32)
                for bk in range(1, REL_BUCKETS):
                    val = jnp.where(d >= starts[bk], tbl_ref[h * REL_BUCKETS + bk], val)
                if off == DIAG:
                    val = jnp.where(d >= 0, val, NEG_INF)
                for mp in range(2):
                    bias_scr[off, :, (2 * h + mp) * tq:(2 * h + mp + 1) * tq] = val

    @pl.when(u == 0)
    def _():
        for c0 in range(0, seq, tk):
            kf = k_ref[c0:c0 + tk, :].astype(F32)
            kn_scr[c0:c0 + tk, :] = (kf * _group_rms(kf, gsum_ref) * gk_ref[...]).astype(BF16)
            vt = v_ref[c0:c0 + tk, :].astype(F32).T.astype(BF16)
            for h in range(N_HEADS):
                vt_scr[h, 0:HEAD_V, c0:c0 + tk] = vt[h * HEAD_V:(h + 1) * HEAD_V, :]
                vt_scr[h, HEAD_V:V_ROWS, c0:c0 + tk] = jnp.ones((V_ROWS - HEAD_V, tk), BF16)

    for p in range(pairs):
        for side, q_ref, r0 in ((0, qa_ref, p * tq), (1, qb_ref, (pairs - 1 - p) * tq)):
            x = 2 * p + side
            qf = q_ref[r0:r0 + tq, :].astype(F32)
            qt = (qf * _group_rms(qf, gsum_ref) * gq_ref[...]).T
            qt = qt.astype(BF16)
            for g in range(N_MAPS):
                rows = slice(g * HEAD_QK, (g + 1) * HEAD_QK)
                qm_scr[x, rows, g * tq:(g + 1) * tq] = qt[rows, :]
    m_scr[...] = jnp.full(m_scr.shape, NEG_INF, F32)
    acc_scr[...] = jnp.zeros(acc_scr.shape, F32)

    def scores(n, x, j, bias_tile, constant_bias):
        sbuf = (s0_scr, s1_scr)[n % 2].at[jnp.minimum(u, 0)]
        c0 = pl.multiple_of(j * tk, tk)
        s = jnp.dot(kn_scr[pl.ds(c0, tk), :], qm_scr[x], preferred_element_type=F32)
        m_old = m_scr[x]
        if constant_bias:
            sbuf[...] = s
            c = cfar_scr[...]
            m_new = jnp.maximum(m_old, jnp.max(s, axis=0, keepdims=True) + c)
            shift = m_new - c
        else:
            s = s + bias_scr[bias_tile]
            sbuf[...] = s
            m_new = jnp.maximum(m_old, jnp.max(s, axis=0, keepdims=True))
            shift = m_new
        m_scr[x] = m_new
        return x, c0, sbuf, shift, jnp.exp2(m_old - m_new)

    def accumulate(x, c0, sbuf, shift, alpha):
        pb = jnp.exp2(sbuf[...] - shift).astype(BF16)
        for h in range(N_HEADS):
            w = slice(2 * h * tq, (2 * h + 2) * tq)
            pv = jnp.dot(vt_scr[h, :, pl.ds(c0, tk)], pb[:, w], preferred_element_type=F32)
            acc_scr[x, h] = alpha[:, w] * acc_scr[x, h] + pv

    blocks = []
    for p in range(pairs):
        t = pairs * u + p
        xa, xb = 2 * p, 2 * p + 1
        qa = t
        qb = nq - 1 - t
        has_sub = t > 0
        n_far_a = jnp.maximum(t - 1, 0)
        blocks += [(xa, qa, DIAG, False), (xb, qb, DIAG, False), (xb, qb - 1, SUB, False),
                   (jnp.where(has_sub, xa, xb), jnp.where(has_sub, qa - 1, nq - 3), jnp.where(has_sub, SUB, FAR),
                    False)]
        for i in range(nq - 3):
            to_a = i < n_far_a
            blocks.append((jnp.where(to_a, xa, xb), jnp.where(to_a, i, i - n_far_a), FAR, True))
    staged = scores(0, *blocks[0])
    for n in range(len(blocks)):
        following = scores(n + 1, *blocks[n + 1]) if n + 1 < len(blocks) else None
        accumulate(*staged)
        staged = following

    lp = lam_ref[...]
    lam = (jnp.exp(jnp.sum(lp[0:1] * lp[1:2], axis=-1, keepdims=True))
           - jnp.exp(jnp.sum(lp[2:3] * lp[3:4], axis=-1, keepdims=True)) + lam_init)
    for p in range(pairs):
        for side in range(2):
            x = 2 * p + side
            heads = []
            for h in range(N_HEADS):
                r = acc_scr[x, h, 0:HEAD_V, :] / acc_scr[x, h, HEAD_V:HEAD_V + 1, :]
                o_h = r[:, :tq] - lam * r[:, tq:]
                ms = jnp.mean(o_h * o_h, axis=0, keepdims=True)
                heads.append(o_h * lax.rsqrt(ms + EPS) * subg_ref[...] * (1.0 - lam_init))
            o_ref[side, p * tq:(p + 1) * tq, :] = jnp.concatenate(heads, axis=0).T.astype(o_ref.dtype)


def _attention(l, qkv, tbl, gq, gk, gsum, lam_params, subg_col, batch, seq, tq, lam_init):
    t = qkv.shape[0]
    nq = seq // tq
    pairs = 2
    assert nq % (2 * pairs) == 0 and nq >= 4
    nh = nq // 2
    steps = nh // pairs
    starts = _bucket_starts(2 * tq)
    assert starts[REL_BUCKETS - 1] <= tq, "blocks two or more tiles below the diagonal must sit in the last bucket"
    whole = functools.partial(_layer_block, l)
    kern = functools.partial(_attn_kernel, tq=tq, nq=nq, pairs=pairs, lam_init=lam_init, starts=starts)
    return pl.pallas_call(
        kern,
        grid=(batch, steps),
        in_specs=[
            pl.BlockSpec(memory_space=pltpu.SMEM),
            pl.BlockSpec((pairs * tq, GROUP_W), lambda b, i: (b * 2 * steps + i, 0)),
            pl.BlockSpec((pairs * tq, GROUP_W), lambda b, i: (b * 2 * steps + 2 * steps - 1 - i, 0)),
            pl.BlockSpec((seq, GROUP_W), lambda b, i: (b, 1)),
            pl.BlockSpec((seq, GROUP_W), lambda b, i: (b, 2)),
            whole((1, GROUP_W)),
            whole((1, GROUP_W)),
            pl.BlockSpec((GROUP_W, GROUP_W), lambda b, i: (0, 0)),
            whole((4, HEAD_QK)),
            whole((HEAD_V, 1)),
        ],
        out_specs=pl.BlockSpec((2, pairs * tq, GROUP_W), lambda b, i: (0, b * steps + i, 0)),
        out_shape=jax.ShapeDtypeStruct((2, t // 2, GROUP_W), BF16),
        scratch_shapes=[
            pltpu.VMEM((seq, GROUP_W), BF16),
            pltpu.VMEM((N_HEADS, V_ROWS, seq), BF16),
            pltpu.VMEM((3, tq, N_MAPS * tq), F32),
            pltpu.VMEM((1, N_MAPS * tq), F32),
            pltpu.VMEM((2 * pairs, GROUP_W, N_MAPS * tq), BF16),
            pltpu.VMEM((2 * pairs, 1, N_MAPS * tq), F32),
            pltpu.VMEM((2 * pairs, N_HEADS, V_ROWS, 2 * tq), F32),
            pltpu.VMEM((1, tq, N_MAPS * tq), F32),
            pltpu.VMEM((1, tq, N_MAPS * tq), F32),
        ],
        compiler_params=pltpu.CompilerParams(
            dimension_semantics=("arbitrary", "arbitrary"), vmem_limit_bytes=VMEM_LIMIT),
        name="diff_attention",
    )(tbl, qkv, qkv, qkv, qkv, gq, gk, gsum, lam_params, subg_col)


def _out_ffn_kernel(x_ref, mix_ref, *rest, n_att, ff_chunk):
    att_refs = rest[:n_att]
    wo_ref, g2_ref, wg_ref, wu_ref, wd_ref, o_ref, h2_scr, a_scr = rest[n_att:]
    mix_w = mix_ref.shape[1]
    att = jnp.concatenate([r[...] for r in att_refs], axis=0)
    o_ref[...] = (x_ref[...]
                  + jnp.dot(mix_ref[...], wo_ref[0:mix_w, :], preferred_element_type=F32)
                  + jnp.dot(att, wo_ref[mix_w:, :], preferred_element_type=F32))
    x1 = o_ref[...]
    ms = jnp.mean(x1 * x1, axis=-1, keepdims=True)
    h2_scr[...] = (o_ref[...] * lax.rsqrt(ms + EPS) * g2_ref[...]).astype(BF16)
    for c0 in range(0, D_FF, ff_chunk):
        gate = jnp.dot(h2_scr[...], wg_ref[:, c0:c0 + ff_chunk], preferred_element_type=F32)
        up = jnp.dot(h2_scr[...], wu_ref[:, c0:c0 + ff_chunk], preferred_element_type=F32)
        a_scr[:, c0:c0 + ff_chunk] = (gate * jax.nn.sigmoid(gate) * up).astype(BF16)
    o_ref[...] += jnp.dot(a_scr[...], wd_ref[...], preferred_element_type=F32)


def _out_ffn(l, x2d, mix, att, wo, g2, wg, wu, wd, tm, seq, tq):
    t = x2d.shape[0]
    nq = seq // tq
    nh = nq // 2
    n_att = tm // tq
    assert tm == n_att * tq and seq % tm == 0
    tiles_per_seq = seq // tm
    whole = functools.partial(_layer_block, l)
    single = pl.Buffered(1)

    def att_block(which):
        def index(i):
            bb = i // tiles_per_seq
            u = n_att * (i % tiles_per_seq) + which
            lower = u < nh
            return (jnp.where(lower, 0, 1), bb * nh + jnp.where(lower, u, nq - 1 - u), 0)
        return pl.BlockSpec((None, tq, GROUP_W), index)

    kern = functools.partial(_out_ffn_kernel, n_att=n_att, ff_chunk=256)
    return pl.pallas_call(
        kern,
        grid=(t // tm,),
        in_specs=[
            pl.BlockSpec((tm, D_MODEL), lambda i: (i, 0)),
            pl.BlockSpec((tm, mix.shape[1]), lambda i: (i, 0)),
            *[att_block(which) for which in range(n_att)],
            whole((D_MODEL, D_MODEL), pipeline_mode=single),
            whole((1, D_MODEL)),
            whole((D_MODEL, D_FF), pipeline_mode=single),
            whole((D_MODEL, D_FF), pipeline_mode=single),
            whole((D_FF, D_MODEL), pipeline_mode=single),
        ],
        out_specs=pl.BlockSpec((tm, D_MODEL), lambda i: (i, 0)),
        out_shape=jax.ShapeDtypeStruct((t, D_MODEL), F32),
        scratch_shapes=[pltpu.VMEM((tm, D_MODEL), BF16),
                        pltpu.VMEM((tm, D_FF), BF16)],
        compiler_params=pltpu.CompilerParams(
            dimension_semantics=("arbitrary",), vmem_limit_bytes=VMEM_LIMIT),
        name="out_ffn",
    )(x2d, mix, *([att] * n_att), wo, g2, wg, wu, wd)


def _block_diag_pool(pool_w):
    out = jnp.zeros((pool_w.shape[0], GROUP_W, GROUP_W), pool_w.dtype)
    for g in range(len(POOL_WINDOWS)):
        out = out.at[:, g * POOL_GROUP:(g + 1) * POOL_GROUP, g * POOL_GROUP:(g + 1) * POOL_GROUP].set(pool_w[:, g])
    return out


def kernel(x, norm1_g, w_in, pool_w, pool_scale, sconv_w, conf_dw_w, conf_dw_b, conf_ln_g, conf_ln_b,
           q_norm_g, k_norm_g, lam_q1, lam_k1, lam_q2, lam_k2, subln_g, w_out, norm2_g, w_gate, w_up,
           w_down, rel_bias):
    batch, seq, d = x.shape
    assert d == D_MODEL and w_in.shape[-1] == D_IN
    depth = w_in.shape[0]
    t = batch * seq
    tq = 256
    tm = 4 * tq
    ts = 1024
    assert seq % tm == 0 and seq % ts == 0 and ts % HALO == 0

    gid = np.arange(GROUP_W) // HEAD_QK
    gsum = jnp.asarray((gid[:, None] == gid[None, :]).astype(np.float32), BF16)
    tbl = (rel_bias.astype(F32).T * LOG2E).reshape(-1)
    rows = lambda v: v.astype(F32)[:, None, :]
    rep8 = lambda v: jnp.broadcast_to(v.astype(F32)[..., None, :], v.shape[:-1] + (8, v.shape[-1]))
    in_mix_params = (rows(norm1_g), w_in.astype(BF16), _block_diag_pool(pool_w).astype(BF16), rows(pool_scale),
                     sconv_w.astype(F32), rep8(conf_dw_w), rep8(conf_dw_b), rep8(conf_ln_g), rep8(conf_ln_b))
    gq = rows(jnp.tile(q_norm_g.astype(F32), (1, N_MAPS)) * (HEAD_QK ** -0.5 * LOG2E))
    gk = rows(jnp.tile(k_norm_g.astype(F32), (1, N_MAPS)))
    lam_params = jnp.stack([lam_q1, lam_k1, lam_q2, lam_k2], axis=1).astype(F32)
    subg_col = subln_g.astype(F32)[:, :, None]
    ffn_params = (w_out.astype(BF16), rows(norm2_g), w_gate.astype(BF16), w_up.astype(BF16), w_down.astype(BF16))

    x2d = x.reshape(t, D_MODEL)
    for l in range(depth):
        lam_init = 0.8 - 0.6 * math.exp(-0.3 * l)
        qkv, mix = _in_mix(l, x2d, *in_mix_params, seq, ts)
        att = _attention(l, qkv, tbl, gq, gk, gsum, lam_params, subg_col, batch, seq, tq, lam_init)
        x2d = _out_ffn(l, x2d, mix, att, *ffn_params, tm, seq, tq)
    return x2d.reshape(batch, seq, D_MODEL)
```
